```python
import jax, jax.numpy as jnp
from jax import lax
import numpy as np

D_MODEL = 1024
BATCH = 16
SEQ = 4096
DEPTH = 4

MIX_WIDTH = D_MODEL
DN_HEADS = 4
DN_HEAD_DIM = MIX_WIDTH // (2 * DN_HEADS)
DN_WIDTH = DN_HEADS * DN_HEAD_DIM
DN_CHUNK = 64
CONV_WIDTH = 4
GM_GROUPS = 4
GM_WIDTH = MIX_WIDTH - DN_WIDTH
GM_GROUP_DIM = GM_WIDTH // GM_GROUPS
GM_CHUNK = 128
OFF_QKV = 0
OFF_Z = 3 * DN_WIDTH
OFF_BETA = 4 * DN_WIDTH
OFF_ALPHA = OFF_BETA + DN_HEADS
OFF_GU = OFF_ALPHA + DN_HEADS
OFF_GV = OFF_GU + GM_WIDTH
D_IN = OFF_GV + GM_WIDTH
D_FF_DENSE = ((8 * D_MODEL // 3 + 127) // 128) * 128
N_EXPERTS = 8
TOP_K = 2
D_FF_EXPERT = 7 * D_MODEL // 2
N_DENSE = (DEPTH + 1) // 2
N_MOE = DEPTH // 2
N_MOD = 6
EPS = 1e-6

kernel_name = 'hybrid_deltanet_gmlp_moe_adaln'


def rms_norm(x, w):
    xf = x.astype(jnp.float32)
    y = xf * lax.rsqrt(jnp.mean(xf * xf, axis=-1, keepdims=True) + EPS)
    return (y * w.astype(jnp.float32)).astype(x.dtype)


def layer_norm(x, w, b):
    xf = x.astype(jnp.float32)
    mu = jnp.mean(xf, axis=-1, keepdims=True)
    var = jnp.mean(jnp.square(xf - mu), axis=-1, keepdims=True)
    y = (xf - mu) * lax.rsqrt(var + EPS)
    return (y * w.astype(jnp.float32) + b.astype(jnp.float32)).astype(x.dtype)


def l2_normalize(x):
    xf = x.astype(jnp.float32)
    return (xf * lax.rsqrt(jnp.sum(xf * xf, axis=-1, keepdims=True) + EPS)).astype(x.dtype)


def modulate(h, shift, scale):
    return h * (1.0 + scale[:, None, :]) + shift[:, None, :]


def causal_depthwise_conv(x, w):
    C = x.shape[-1]
    return lax.conv_general_dilated(
        x, w[:, None, :].astype(x.dtype), window_strides=(1,),
        padding=[(CONV_WIDTH - 1, 0)], dimension_numbers=('NWC', 'WIO', 'NWC'),
        feature_group_count=C)


def gated_delta_rule(q, k, v, g, beta):
    B, L, H, Dk = q.shape
    Dv = v.shape[-1]
    C = DN_CHUNK
    N = L // C
    f32 = jnp.float32

    def to_chunks(t):
        t = t.astype(f32).reshape(B, N, C, H, *t.shape[3:])
        return jnp.moveaxis(t, (1, 3), (0, 2))

    q, k, v = to_chunks(q), to_chunks(k), to_chunks(v)
    g, beta = to_chunks(g), to_chunks(beta)
    g_cum = jnp.cumsum(g, axis=-1)
    tri = jnp.tril(jnp.ones((C, C), dtype=bool))
    strict = jnp.tril(jnp.ones((C, C), dtype=bool), -1)
    diff = g_cum[..., :, None] - g_cum[..., None, :]
    decay = jnp.exp(jnp.where(tri, diff, -jnp.inf))
    k_beta = k * beta[..., None]
    v_beta = v * beta[..., None]
    a_mat = jnp.where(strict, jnp.einsum('nbhcd,nbhsd->nbhcs', k_beta, k) * decay, 0.0)
    eye = jnp.eye(C, dtype=f32)
    t_mat = lax.linalg.triangular_solve(a_mat + eye, jnp.broadcast_to(eye, a_mat.shape),
                                        left_side=True, lower=True)
    u = jnp.einsum('nbhcs,nbhse->nbhce', t_mat, v_beta)
    w = jnp.einsum('nbhcs,nbhsd->nbhcd', t_mat, k_beta * jnp.exp(g_cum)[..., None])
    attn = jnp.einsum('nbhcd,nbhsd->nbhcs', q, k) * decay
    q_dec = q * jnp.exp(g_cum)[..., None]
    g_last = g_cum[..., -1]
    k_dec = k * jnp.exp(g_last[..., None] - g_cum)[..., None]

    def step(S, xs):
        u_n, w_n, q_n, k_n, attn_n, gl_n = xs
        v_new = u_n - jnp.einsum('bhcd,bhde->bhce', w_n, S)
        o_n = (jnp.einsum('bhcd,bhde->bhce', q_n, S)
               + jnp.einsum('bhcs,bhse->bhce', attn_n, v_new))
        S = S * jnp.exp(gl_n)[..., None, None] + jnp.einsum('bhcd,bhce->bhde', k_n, v_new)
        return S, o_n

    S0 = jnp.zeros((B, H, Dk, Dv), f32)
    _, o = lax.scan(step, S0, (u, w, q_dec, k_dec, attn, g_last))
    return jnp.moveaxis(o, (0, 2), (1, 3)).reshape(B, L, H, Dv)


def hybrid_mixer(h, w_in, conv_w, a_log, dt_bias, dn_norm_w, gm_ln_w, gm_ln_b,
                 gm_spatial_w, gm_spatial_b, w_out):
    B, L, _ = h.shape
    proj = h @ w_in
    qkv = jax.nn.silu(causal_depthwise_conv(proj[..., OFF_QKV:OFF_Z], conv_w))
    q = qkv[..., 0:DN_WIDTH].reshape(B, L, DN_HEADS, DN_HEAD_DIM)
    k = qkv[..., DN_WIDTH:2 * DN_WIDTH].reshape(B, L, DN_HEADS, DN_HEAD_DIM)
    v = qkv[..., 2 * DN_WIDTH:3 * DN_WIDTH].reshape(B, L, DN_HEADS, DN_HEAD_DIM)
    z = proj[..., OFF_Z:OFF_BETA].reshape(B, L, DN_HEADS, DN_HEAD_DIM)
    beta = jax.nn.sigmoid(proj[..., OFF_BETA:OFF_ALPHA].astype(jnp.float32))
    alpha = proj[..., OFF_ALPHA:OFF_GU].astype(jnp.float32)
    g = -jnp.exp(a_log.astype(jnp.float32)) * jax.nn.softplus(alpha + dt_bias.astype(jnp.float32))
    q = l2_normalize(q) * (DN_HEAD_DIM ** -0.5)
    k = l2_normalize(k)
    o = gated_delta_rule(q, k, v, g, beta).astype(h.dtype)
    o_dn = (rms_norm(o, dn_norm_w) * jax.nn.silu(z)).reshape(B, L, DN_WIDTH)
    gu = jax.nn.gelu(proj[..., OFF_GU:OFF_GV])
    gv = layer_norm(jax.nn.gelu(proj[..., OFF_GV:D_IN]), gm_ln_w, gm_ln_b)
    n_chunks = L // GM_CHUNK
    gv = gv.reshape(B, n_chunks, GM_CHUNK, GM_GROUPS, GM_GROUP_DIM)
    causal = jnp.tril(jnp.ones((GM_CHUNK, GM_CHUNK), dtype=bool))
    w_s = jnp.where(causal, gm_spatial_w, 0.0).astype(h.dtype)
    s = jnp.einsum('gts,bnsgd->bntgd', w_s, gv) + gm_spatial_b.T[None, None, :, :, None]
    o_gm = (gu.reshape(B, n_chunks, GM_CHUNK, GM_GROUPS, GM_GROUP_DIM) * s).reshape(B, L, GM_WIDTH)
    return jnp.concatenate([o_dn, o_gm], axis=-1) @ w_out


def swiglu(h, wg, wu, wd):
    return (jax.nn.silu(h @ wg) * (h @ wu)) @ wd


def moe_swiglu(h, w_router, w_gate, w_up, w_down):
    logits = (h @ w_router).astype(jnp.float32)
    top_vals, top_idx = lax.top_k(logits, TOP_K)
    top_w = jax.nn.softmax(top_vals, axis=-1)
    combine = jnp.sum(jax.nn.one_hot(top_idx, N_EXPERTS, dtype=jnp.float32) * top_w[..., None],
                      axis=-2).astype(h.dtype)
    y = jnp.zeros_like(h)
    for e in range(N_EXPERTS):
        y = y + combine[..., e:e + 1] * swiglu(h, w_gate[e], w_up[e], w_down[e])
    return y


def setup_inputs(seed: int = 0) -> dict:
    key = jax.random.key(seed)
    ks = jax.random.split(key, 32)
    f32 = jnp.float32
    D = D_MODEL

    def nrm(k, shape, fan_in, gain=1.0):
        return jax.random.normal(k, shape, f32) * (gain * fan_in ** -0.5)

    x = jax.random.normal(ks[0], (BATCH, SEQ, D), f32)
    c = jax.random.normal(ks[1], (BATCH, D), f32)
    ada_w = nrm(ks[2], (DEPTH, D, N_MOD * D), D, 0.5)
    ada_b = 0.02 * jax.random.normal(ks[3], (DEPTH, N_MOD * D), f32)
    norm1_w = 1.0 + 0.05 * jax.random.normal(ks[4], (DEPTH, D), f32)
    norm2_w = 1.0 + 0.05 * jax.random.normal(ks[5], (DEPTH, D), f32)
    w_in = nrm(ks[6], (DEPTH, D, D_IN), D)
    conv_w = nrm(ks[7], (DEPTH, CONV_WIDTH, 3 * DN_WIDTH), CONV_WIDTH)
    a_log = jnp.log(jax.random.uniform(ks[8], (DEPTH, DN_HEADS), f32, 1.0, 16.0))
    dt = jnp.exp(jax.random.uniform(ks[9], (DEPTH, DN_HEADS), f32, np.log(1e-3), np.log(1e-1)))
    dt_bias = dt + jnp.log(-jnp.expm1(-dt))
    dn_norm_w = 1.0 + 0.05 * jax.random.normal(ks[10], (DEPTH, DN_HEAD_DIM), f32)
    gm_ln_w = 1.0 + 0.05 * jax.random.normal(ks[11], (DEPTH, GM_WIDTH), f32)
    gm_ln_b = 0.02 * jax.random.normal(ks[12], (DEPTH, GM_WIDTH), f32)
    gm_spatial_w = nrm(ks[13], (DEPTH, GM_GROUPS, GM_CHUNK, GM_CHUNK), GM_CHUNK, 0.5)
    gm_spatial_b = 1.0 + 0.1 * jax.random.normal(ks[14], (DEPTH, GM_GROUPS, GM_CHUNK), f32)
    w_out = nrm(ks[15], (DEPTH, MIX_WIDTH, D), MIX_WIDTH)
    ffn_w_gate = nrm(ks[16], (N_DENSE, D, D_FF_DENSE), D)
    ffn_w_up = nrm(ks[17], (N_DENSE, D, D_FF_DENSE), D)
    ffn_w_down = nrm(ks[18], (N_DENSE, D_FF_DENSE, D), D_FF_DENSE)
    moe_router = nrm(ks[19], (N_MOE, D, N_EXPERTS), D)
    moe_w_gate = nrm(ks[20], (N_MOE, N_EXPERTS, D, D_FF_EXPERT), D)
    moe_w_up = nrm(ks[21], (N_MOE, N_EXPERTS, D, D_FF_EXPERT), D)
    moe_w_down = nrm(ks[22], (N_MOE, N_EXPERTS, D_FF_EXPERT, D), D_FF_EXPERT)
    final_ada_w = nrm(ks[23], (D, 2 * D), D, 0.5)
    final_ada_b = 0.02 * jax.random.normal(ks[24], (2 * D,), f32)
    final_norm_w = 1.0 + 0.05 * jax.random.normal(ks[25], (D,), f32)
    return {'x': x, 'c': c, 'ada_w': ada_w, 'ada_b': ada_b, 'norm1_w': norm1_w,
            'norm2_w': norm2_w, 'w_in': w_in, 'conv_w': conv_w, 'a_log': a_log,
            'dt_bias': dt_bias, 'dn_norm_w': dn_norm_w, 'gm_ln_w': gm_ln_w,
            'gm_ln_b': gm_ln_b, 'gm_spatial_w': gm_spatial_w, 'gm_spatial_b': gm_spatial_b,
            'w_out': w_out, 'ffn_w_gate': ffn_w_gate, 'ffn_w_up': ffn_w_up,
            'ffn_w_down': ffn_w_down, 'moe_router': moe_router, 'moe_w_gate': moe_w_gate,
            'moe_w_up': moe_w_up, 'moe_w_down': moe_w_down, 'final_ada_w': final_ada_w,
            'final_ada_b': final_ada_b, 'final_norm_w': final_norm_w}


def reference(x, c, ada_w, ada_b, norm1_w, norm2_w, w_in, conv_w, a_log, dt_bias,
              dn_norm_w, gm_ln_w, gm_ln_b, gm_spatial_w, gm_spatial_b, w_out,
              ffn_w_gate, ffn_w_up, ffn_w_down, moe_router, moe_w_gate, moe_w_up,
              moe_w_down, final_ada_w, final_ada_b, final_norm_w):
    c_act = jax.nn.silu(c)
    for i in range(DEPTH):
        mod = c_act @ ada_w[i] + ada_b[i]
        sh1, sc1, g1, sh2, sc2, g2 = jnp.split(mod, N_MOD, axis=-1)
        h = modulate(rms_norm(x, norm1_w[i]), sh1, sc1)
        mix = hybrid_mixer(h, w_in[i], conv_w[i], a_log[i], dt_bias[i], dn_norm_w[i],
                           gm_ln_w[i], gm_ln_b[i], gm_spatial_w[i], gm_spatial_b[i], w_out[i])
        x = x + g1[:, None, :] * mix
        h = modulate(rms_norm(x, norm2_w[i]), sh2, sc2)
        j = i // 2
        if i % 2 == 0:
            f = swiglu(h, ffn_w_gate[j], ffn_w_up[j], ffn_w_down[j])
        else:
            f = moe_swiglu(h, moe_router[j], moe_w_gate[j], moe_w_up[j], moe_w_down[j])
        x = x + g2[:, None, :] * f
    fmod = c_act @ final_ada_w + final_ada_b
    f_shift, f_scale = jnp.split(fmod, 2, axis=-1)
    return modulate(rms_norm(x, final_norm_w), f_shift, f_scale)
```

```python
import functools
import math

import jax
import jax.numpy as jnp
from jax import lax
from jax.experimental import pallas as pl
from jax.experimental.pallas import tpu as pltpu

F32 = jnp.float32
BF16 = jnp.bfloat16

EPS = 1e-6
D_MODEL = 1024
DN_HEADS = 4
HEAD_DIM = 128
DN_WIDTH = DN_HEADS * HEAD_DIM
CONV_WIDTH = 4
GM_GROUPS = 4
GM_WIDTH = 512
GM_CHUNK = 128
DN_CHUNK = 128
N_EXPERTS = 8
LANES = 128
C_QKV, C_Z, C_GU, C_GV, C_BA, C_END = 0, 1536, 2048, 2560, 3072, 3200

VMEM_LIMIT = 56 * 1024 * 1024


def _cparams(sem):
    return pltpu.CompilerParams(dimension_semantics=sem, vmem_limit_bytes=VMEM_LIMIT)


def _sigmoid(x):
    return 1.0 / (1.0 + jnp.exp(-x))


def _silu(x):
    return x * _sigmoid(x)


def _gelu_tanh(x):
    return 0.5 * x * (1.0 + jnp.tanh(math.sqrt(2.0 / math.pi) * (x + 0.044715 * (x * x * x))))


def _softplus(x):
    return jnp.maximum(x, 0.0) + jnp.log(1.0 + jnp.exp(-jnp.abs(x)))


def _split_hi_lo(x):
    hi = x.astype(BF16)
    lo = (x - hi.astype(F32)).astype(BF16)
    return hi, lo


def _dot(a, b):
    return jnp.dot(a, b, preferred_element_type=F32)


def _ada_kernel(c_ref, w_ref, b_ref, o_ref):
    c = c_ref[...]
    ca = _silu(c)
    c_hi, c_lo = _split_hi_lo(ca)
    w = w_ref[0]
    w_hi, w_lo = _split_hi_lo(w)
    o_ref[0] = _dot(c_hi, w_hi) + _dot(c_lo, w_hi) + _dot(c_hi, w_lo) + b_ref[0]


def ada_mod(c, w, b, tn=2048):
    nl, d, n = w.shape
    bsz = c.shape[0]
    return pl.pallas_call(
        _ada_kernel,
        grid=(nl, n // tn),
        in_specs=[pl.BlockSpec((bsz, d), lambda l, j: (0, 0)),
                  pl.BlockSpec((1, d, tn), lambda l, j: (l, 0, j)),
                  pl.BlockSpec((1, 1, tn), lambda l, j: (l, 0, j))],
        out_specs=pl.BlockSpec((1, bsz, tn), lambda l, j: (l, 0, j)),
        out_shape=jax.ShapeDtypeStruct((nl, bsz, n), F32),
        compiler_params=_cparams(("arbitrary", "arbitrary")),
        name="ada_mod",
    )(c, w, b)


def _norm_mod(x, nw, sh, sc):
    ms = jnp.mean(x * x, axis=-1, keepdims=True)
    return (x * lax.rsqrt(ms + EPS) * nw) * (1.0 + sc) + sh


def _inproj_kernel(x_ref, nw_ref, sh_ref, sc_ref, w_ref, cw_ref, hp_ref, lnw_ref, lnb_ref,
                   ws_ref, bs_ref,
                   q_ref, kb_ref, vb_ref, kt_ref, zs_ref, gc_ref, gct_ref, ogm_ref,
                   halo_ref, *, tm, tiles_per_seq):
    i = pl.program_id(0)
    x = x_ref[...]
    hb = _norm_mod(x, nw_ref[...], sh_ref[0], sc_ref[0]).astype(BF16)

    pre = _dot(hb, w_ref[:, C_QKV:C_Z])
    @pl.when((i % tiles_per_seq) == 0)
    def _():
        halo_ref[...] = jnp.zeros_like(halo_ref)

    halo = halo_ref[...]
    halo_ref[...] = pre[tm - 8:tm, :]
    cw = cw_ref[...]
    row8 = lax.broadcasted_iota(jnp.int32, (8, pre.shape[1]), 0)
    conv = pre * cw[CONV_WIDTH - 1:CONV_WIDTH, :]
    for s in range(1, CONV_WIDTH):
        sh = pltpu.roll(pre, s, 0)
        fix = pltpu.roll(halo, s, 0)
        top = jnp.where(row8 < s, fix, sh[0:8, :])
        sh = jnp.concatenate([top, sh[8:, :]], axis=0)
        conv = conv + sh * cw[CONV_WIDTH - 1 - s:CONV_WIDTH - s, :]
    act = _silu(conv)

    ba = _dot(hb, w_ref[:, C_BA:C_END])
    beta = _sigmoid(ba)
    hp = hp_ref[...]
    g = hp[0:1, :] * _softplus(ba + hp[1:2, :])
    rowc = lax.broadcasted_iota(jnp.int32, (tm, LANES), 0) % DN_CHUNK
    gc = g
    s = 1
    while s < DN_CHUNK:
        gc = gc + jnp.where(rowc >= s, pltpu.roll(gc, s, 0), 0.0)
        s *= 2
    gc_ref[...] = gc
    gct = gc.T
    for j in range(tm // DN_CHUNK):
        gct_ref[j] = gct[0:8, j * DN_CHUNK:(j + 1) * DN_CHUNK]

    kn_all = []
    for h in range(DN_HEADS):
        lo, hi = h * HEAD_DIM, (h + 1) * HEAD_DIM
        qh = act[:, lo:hi]
        kh = act[:, DN_WIDTH + lo:DN_WIDTH + hi]
        vh = act[:, 2 * DN_WIDTH + lo:2 * DN_WIDTH + hi]
        qn = qh * lax.rsqrt(jnp.sum(qh * qh, axis=-1, keepdims=True) + EPS) * (HEAD_DIM ** -0.5)
        kn = kh * lax.rsqrt(jnp.sum(kh * kh, axis=-1, keepdims=True) + EPS)
        bh = beta[:, h:h + 1]
        q_ref[:, lo:hi] = qn.astype(BF16)
        kb_ref[:, lo:hi] = (kn * bh).astype(BF16)
        vb_ref[:, lo:hi] = (vh * bh).astype(BF16)
        kn_all.append(kn)
    kt = jnp.concatenate(kn_all, axis=1).T
    for j in range(tm // DN_CHUNK):
        kt_ref[j] = kt[:, j * DN_CHUNK:(j + 1) * DN_CHUNK].astype(BF16)

    zs_ref[...] = _silu(_dot(hb, w_ref[:, C_Z:C_GU])).astype(BF16)

    gu = _gelu_tanh(_dot(hb, w_ref[:, C_GU:C_GV]))
    gv = _gelu_tanh(_dot(hb, w_ref[:, C_GV:C_BA]))
    mu = jnp.mean(gv, axis=-1, keepdims=True)
    gvc = gv - mu
    var = jnp.mean(gvc * gvc, axis=-1, keepdims=True)
    gvn = (gvc * lax.rsqrt(var + EPS) * lnw_ref[...] + lnb_ref[...]).astype(BF16)
    for j in range(tm // GM_CHUNK):
        r0, r1 = j * GM_CHUNK, (j + 1) * GM_CHUNK
        for gi in range(GM_GROUPS):
            c0, c1 = gi * LANES, (gi + 1) * LANES
            sp = _dot(ws_ref[gi], gvn[r0:r1, c0:c1]) + bs_ref[gi]
            ogm_ref[r0:r1, c0:c1] = (gu[r0:r1, c0:c1] * sp).astype(BF16)


def inproj(x2, nw, sh, sc, w, cw, hp, lnw, lnb, ws, bs, *, seq, tm=512):
    t, d = x2.shape
    nt = t // tm
    tiles_per_seq = seq // tm
    nc = tm // DN_CHUNK
    row = lambda i: (i, 0)
    const2 = lambda i: (0, 0)
    const3 = lambda i: (0, 0, 0)
    bidx = lambda i: (i // tiles_per_seq, 0, 0)
    out_shape = [
        jax.ShapeDtypeStruct((t, DN_WIDTH), BF16),
        jax.ShapeDtypeStruct((t, DN_WIDTH), BF16),
        jax.ShapeDtypeStruct((t, DN_WIDTH), BF16),
        jax.ShapeDtypeStruct((t // DN_CHUNK, DN_WIDTH, DN_CHUNK), BF16),
        jax.ShapeDtypeStruct((t, DN_WIDTH), BF16),
        jax.ShapeDtypeStruct((t, LANES), F32),
        jax.ShapeDtypeStruct((t // DN_CHUNK, 8, DN_CHUNK), F32),
        jax.ShapeDtypeStruct((t, GM_WIDTH), BF16),
    ]
    out_specs = [
        pl.BlockSpec((tm, DN_WIDTH), row),
        pl.BlockSpec((tm, DN_WIDTH), row),
        pl.BlockSpec((tm, DN_WIDTH), row),
        pl.BlockSpec((nc, DN_WIDTH, DN_CHUNK), lambda i: (i, 0, 0)),
        pl.BlockSpec((tm, DN_WIDTH), row),
        pl.BlockSpec((tm, LANES), row),
        pl.BlockSpec((nc, 8, DN_CHUNK), lambda i: (i, 0, 0)),
        pl.BlockSpec((tm, GM_WIDTH), row),
    ]
    in_specs = [
        pl.BlockSpec((tm, d), row),
        pl.BlockSpec((1, d), const2),
        pl.BlockSpec((1, 1, d), bidx),
        pl.BlockSpec((1, 1, d), bidx),
        pl.BlockSpec((d, C_END), const2),
        pl.BlockSpec((CONV_WIDTH, 3 * DN_WIDTH), const2),
        pl.BlockSpec((8, LANES), const2),
        pl.BlockSpec((1, GM_WIDTH), const2),
        pl.BlockSpec((1, GM_WIDTH), const2),
        pl.BlockSpec((GM_GROUPS, GM_CHUNK, GM_CHUNK), const3),
        pl.BlockSpec((GM_GROUPS, GM_CHUNK, LANES), const3),
    ]
    return pl.pallas_call(
        functools.partial(_inproj_kernel, tm=tm, tiles_per_seq=tiles_per_seq),
        grid=(nt,),
        in_specs=in_specs,
        out_specs=out_specs,
        out_shape=out_shape,
        scratch_shapes=[pltpu.VMEM((8, 3 * DN_WIDTH), F32)],
        compiler_params=_cparams(("arbitrary",)),
        name="inproj",
    )(x2, nw, sh, sc, w, cw, hp, lnw, lnb, ws, bs)


INV_BASE = 8


def _unit_lower_inverse(a, ri, ci):
    c = a.shape[0]
    same = (ri // INV_BASE) == (ci // INV_BASE)
    ab = jnp.where(same, a, 0.0).astype(BF16)
    x = jnp.where(ri == ci, 1.0, 0.0) - jnp.where(same, a, 0.0)
    p = _dot(ab, ab)
    n = 2
    while True:
        pb = p.astype(BF16)
        x = x + _dot(x.astype(BF16), pb)
        n *= 2
        if n >= INV_BASE:
            break
        p = _dot(pb, pb)
    s = INV_BASE
    while s < c:
        off = jnp.logical_and((ri // (2 * s)) == (ci // (2 * s)), (ri // s) != (ci // s))
        xb = x.astype(BF16)
        x = x - _dot(_dot(xb, jnp.where(off, a, 0.0).astype(BF16)).astype(BF16), xb)
        s *= 2
    return x


def _deltanet_kernel(q_ref, kb_ref, vb_ref, kt_ref, zs_ref, gc_ref, gct_ref, nw_ref,
                     o_ref, s_ref, u_ref, w_ref, attn_ref, *, tl, unroll):
    l = pl.program_id(1)

    @pl.when(l == 0)
    def _():
        s_ref[...] = jnp.zeros_like(s_ref)

    c = DN_CHUNK
    ri = lax.broadcasted_iota(jnp.int32, (c, c), 0)
    ci = lax.broadcasted_iota(jnp.int32, (c, c), 1)
    tri = ri >= ci
    strict = ri > ci
    nw = nw_ref[...]

    def prep(it, carry):
        for sub in range(unroll):
            ch = it * unroll + sub
            r0 = pl.multiple_of(ch * c, c)
            gc = gc_ref[pl.ds(r0, c), :]
            gct = gct_ref[ch]
            for h in range(DN_HEADS):
                lo, hi = h * HEAD_DIM, (h + 1) * HEAD_DIM
                q = q_ref[pl.ds(r0, c), lo:hi]
                kb = kb_ref[pl.ds(r0, c), lo:hi]
                vb = vb_ref[pl.ds(r0, c), lo:hi]
                kt = kt_ref[ch, lo:hi, :]
                gcol = jnp.broadcast_to(gc[:, 4 + h:5 + h], (c, c))
                grow = jnp.broadcast_to(gct[4 + h:5 + h, :], (c, c))
                decay = jnp.where(tri, jnp.exp(jnp.minimum(gcol - grow, 0.0)), 0.0)
                a = jnp.where(strict, _dot(kb, kt) * decay, 0.0)
                attn_ref[ch, h] = (_dot(q, kt) * decay).astype(BF16)
                tb = _unit_lower_inverse(a, ri, ci).astype(BF16)
                u_ref[pl.ds(r0, c), lo:hi] = _dot(tb, vb)
                w_ref[pl.ds(r0, c), lo:hi] = _dot(
                    tb, (kb.astype(F32) * jnp.exp(gcol)).astype(BF16)).astype(BF16)
        return carry

    lax.fori_loop(0, tl // (c * unroll), prep, 0)

    def step(ch, carry):
        r0 = pl.multiple_of(ch * c, c)
        gc = gc_ref[pl.ds(r0, c), :]
        for h in range(DN_HEADS):
            lo, hi = h * HEAD_DIM, (h + 1) * HEAD_DIM
            q = q_ref[pl.ds(r0, c), lo:hi]
            kt = kt_ref[ch, lo:hi, :]
            gcol = jnp.broadcast_to(gc[:, 4 + h:5 + h], (c, c))
            glast = jnp.broadcast_to(gcol[c - 1:c, :], (c, c))
            sb = s_ref[h].astype(BF16)
            v_new = u_ref[pl.ds(r0, c), lo:hi] - _dot(w_ref[pl.ds(r0, c), lo:hi], sb)
            o = _dot(q, sb) * jnp.exp(gcol) + _dot(attn_ref[ch, h], v_new.astype(BF16))
            s_ref[h] = s_ref[h] * jnp.exp(glast) + _dot(
                kt, (v_new * jnp.exp(glast - gcol)).astype(BF16))
            ms = jnp.mean(o * o, axis=-1, keepdims=True)
            on = o * lax.rsqrt(ms + EPS) * nw
            zs = zs_ref[pl.ds(r0, c), lo:hi].astype(F32)
            o_ref[pl.ds(r0, c), lo:hi] = (on * zs).astype(BF16)
        return carry

    lax.fori_loop(0, tl // c, step, 0)


def deltanet(q, kb, vb, kt, zs, gc, gct, nw, *, batch, seq, tl=512):
    t = q.shape[0]
    nl = seq // tl
    nc = tl // DN_CHUNK
    row = lambda b, l: (b * nl + l, 0)
    row3 = lambda b, l: (b * nl + l, 0, 0)
    return pl.pallas_call(
        functools.partial(_deltanet_kernel, tl=tl, unroll=2),
        grid=(batch, nl),
        in_specs=[pl.BlockSpec((tl, DN_WIDTH), row),
                  pl.BlockSpec((tl, DN_WIDTH), row),
                  pl.BlockSpec((tl, DN_WIDTH), row),
                  pl.BlockSpec((nc, DN_WIDTH, DN_CHUNK), row3),
                  pl.BlockSpec((tl, DN_WIDTH), row),
                  pl.BlockSpec((tl, LANES), row),
                  pl.BlockSpec((nc, 8, DN_CHUNK), row3),
                  pl.BlockSpec((1, HEAD_DIM), lambda b, l: (0, 0))],
        out_specs=pl.BlockSpec((tl, DN_WIDTH), row),
        out_shape=jax.ShapeDtypeStruct((t, DN_WIDTH), BF16),
        scratch_shapes=[pltpu.VMEM((DN_HEADS, HEAD_DIM, HEAD_DIM), F32),
                        pltpu.VMEM((tl, DN_WIDTH), F32),
                        pltpu.VMEM((tl, DN_WIDTH), BF16),
                        pltpu.VMEM((nc, DN_HEADS, DN_CHUNK, DN_CHUNK), BF16)],
        compiler_params=_cparams(("arbitrary", "arbitrary")),
        name="deltanet",
    )(q, kb, vb, kt, zs, gc, gct, nw)


def _outproj_kernel(odn_ref, ogm_ref, w_ref, x_ref, g_ref, o_ref):
    mix = _dot(odn_ref[...], w_ref[0:DN_WIDTH, :]) + _dot(ogm_ref[...], w_ref[DN_WIDTH:, :])
    o_ref[...] = x_ref[...] + g_ref[0] * mix


def outproj(odn, ogm, w, x2, g1, *, seq, tm=1024):
    t, d = x2.shape
    tiles_per_seq = seq // tm
    row = lambda i: (i, 0)
    return pl.pallas_call(
        _outproj_kernel,
        grid=(t // tm,),
        in_specs=[pl.BlockSpec((tm, DN_WIDTH), row),
                  pl.BlockSpec((tm, GM_WIDTH), row),
                  pl.BlockSpec((d, d), lambda i: (0, 0)),
                  pl.BlockSpec((tm, d), row),
                  pl.BlockSpec((1, 1, d), lambda i: (i // tiles_per_seq, 0, 0))],
        out_specs=pl.BlockSpec((tm, d), row),
        out_shape=jax.ShapeDtypeStruct((t, d), F32),
        compiler_params=_cparams(("arbitrary",)),
        name="outproj",
    )(odn, ogm, w, x2, g1)


def _ffn_kernel(x_ref, nw_ref, sh_ref, sc_ref, g_ref, wg_ref, wu_ref, wd_ref, o_ref,
                hb_ref, acc_ref):
    f = pl.program_id(1)

    @pl.when(f == 0)
    def _():
        hb_ref[...] = _norm_mod(x_ref[...], nw_ref[...], sh_ref[0], sc_ref[0]).astype(BF16)
        acc_ref[...] = jnp.zeros_like(acc_ref)

    hb = hb_ref[...]
    gate = _dot(hb, wg_ref[...])
    up = _dot(hb, wu_ref[...])
    acc_ref[...] += _dot((_silu(gate) * up).astype(BF16), wd_ref[...])

    @pl.when(f == pl.num_programs(1) - 1)
    def _():
        o_ref[...] = x_ref[...] + g_ref[0] * acc_ref[...]


def ffn(x2, nw, sh, sc, g2, wg, wu, wd, *, seq, tm=1024, tf=256):
    t, d = x2.shape
    dff = wg.shape[1]
    tiles_per_seq = seq // tm
    row = lambda i, f: (i, 0)
    bidx = lambda i, f: (i // tiles_per_seq, 0, 0)
    return pl.pallas_call(
        _ffn_kernel,
        grid=(t // tm, dff // tf),
        in_specs=[pl.BlockSpec((tm, d), row),
                  pl.BlockSpec((1, d), lambda i, f: (0, 0)),
                  pl.BlockSpec((1, 1, d), bidx),
                  pl.BlockSpec((1, 1, d), bidx),
                  pl.BlockSpec((1, 1, d), bidx),
                  pl.BlockSpec((d, tf), lambda i, f: (0, f)),
                  pl.BlockSpec((d, tf), lambda i, f: (0, f)),
                  pl.BlockSpec((tf, d), lambda i, f: (f, 0))],
        out_specs=pl.BlockSpec((tm, d), row),
        out_shape=jax.ShapeDtypeStruct((t, d), F32),
        scratch_shapes=[pltpu.VMEM((tm, d), BF16), pltpu.VMEM((tm, d), F32)],
        compiler_params=_cparams(("arbitrary", "arbitrary")),
        name="ffn",
    )(x2, nw, sh, sc, g2, wg, wu, wd)


def _router_kernel(x_ref, nw_ref, sh_ref, sc_ref, wr_ref, h_ref, meta_ref, cnt_ref,
                   carry_ref, *, tm):
    i = pl.program_id(0)

    @pl.when(i == 0)
    def _():
        carry_ref[...] = jnp.zeros_like(carry_ref)

    h = _norm_mod(x_ref[...], nw_ref[...], sh_ref[0], sc_ref[0])
    h_ref[...] = h
    h_hi, h_lo = _split_hi_lo(h)
    w_hi, w_lo = _split_hi_lo(wr_ref[...])
    logits = _dot(h_hi, w_hi) + _dot(h_lo, w_hi) + _dot(h_hi, w_lo)
    lane = lax.broadcasted_iota(jnp.int32, (tm, LANES), 1).astype(F32)
    neg = jnp.float32(-jnp.inf)
    lg = jnp.where(lane < N_EXPERTS, logits, neg)
    m1 = jnp.max(lg, axis=-1, keepdims=True)
    i1 = jnp.min(jnp.where(lg == m1, lane, float(LANES)), axis=-1, keepdims=True)
    lg2 = jnp.where(lane == i1, neg, lg)
    m2 = jnp.max(lg2, axis=-1, keepdims=True)
    i2 = jnp.min(jnp.where(lg2 == m2, lane, float(LANES)), axis=-1, keepdims=True)
    e2 = jnp.exp(m2 - m1)
    w1 = 1.0 / (1.0 + e2)
    w2 = e2 / (1.0 + e2)
    oh1 = lane == i1
    oh2 = lane == i2
    onehot = jnp.where(oh1 | oh2, 1.0, 0.0)
    ri = lax.broadcasted_iota(jnp.int32, (tm, tm), 0)
    ci = lax.broadcasted_iota(jnp.int32, (tm, tm), 1)
    lower = jnp.where(ri > ci, 1.0, 0.0).astype(BF16)
    before = _dot(lower, onehot.astype(BF16)) + carry_ref[0:1, :]
    r1 = jnp.sum(jnp.where(oh1, before, 0.0), axis=-1, keepdims=True)
    r2 = jnp.sum(jnp.where(oh2, before, 0.0), axis=-1, keepdims=True)
    total = carry_ref[0:1, :] + jnp.sum(onehot, axis=0, keepdims=True)
    carry_ref[...] = jnp.broadcast_to(total, carry_ref.shape)
    cnt_ref[...] = jnp.broadcast_to(total, cnt_ref.shape)
    meta = jnp.where(lane == 0, i1, 0.0)
    meta = jnp.where(lane == 1, i2, meta)
    meta = jnp.where(lane == 2, w1, meta)
    meta = jnp.where(lane == 3, w2, meta)
    meta = jnp.where(lane == 4, r1, meta)
    meta = jnp.where(lane == 5, r2, meta)
    meta_ref[...] = meta


def router(x2, nw, sh, sc, wr, *, seq, tm=512):
    t, d = x2.shape
    tiles_per_seq = seq // tm
    row = lambda i: (i, 0)
    bidx = lambda i: (i // tiles_per_seq, 0, 0)
    return pl.pallas_call(
        functools.partial(_router_kernel, tm=tm),
        grid=(t // tm,),
        in_specs=[pl.BlockSpec((tm, d), row),
                  pl.BlockSpec((1, d), lambda i: (0, 0)),
                  pl.BlockSpec((1, 1, d), bidx),
                  pl.BlockSpec((1, 1, d), bidx),
                  pl.BlockSpec((d, LANES), lambda i: (0, 0))],
        out_specs=[pl.BlockSpec((tm, d), row),
                   pl.BlockSpec((tm, LANES), row),
                   pl.BlockSpec((8, LANES), lambda i: (0, 0))],
        out_shape=[jax.ShapeDtypeStruct((t, d), F32),
                   jax.ShapeDtypeStruct((t, LANES), F32),
                   jax.ShapeDtypeStruct((8, LANES), F32)],
        scratch_shapes=[pltpu.VMEM((8, LANES), F32)],
        compiler_params=_cparams(("arbitrary",)),
        name="router",
    )(x2, nw, sh, sc, wr)


def _dispatch_kernel(dest_ref, h_ref, xs_in_ref, xs_ref, sem, *, tm):
    del xs_in_ref

    def issue(r, carry):
        for k in range(2):
            d = dest_ref[0, 0, 2 * r + k]
            pltpu.make_async_copy(h_ref.at[pl.ds(r, 1), :], xs_ref.at[pl.ds(d, 1), :], sem).start()
        return carry

    lax.fori_loop(0, tm, issue, 0)

    def drain(r, carry):
        for k in range(2):
            pltpu.make_async_copy(h_ref.at[pl.ds(0, 1), :], xs_ref.at[pl.ds(0, 1), :], sem).wait()
        return carry

    lax.fori_loop(0, tm, drain, 0)


def dispatch(h, dest3, xs_init, *, tm=512):
    t, d = h.shape
    return pl.pallas_call(
        functools.partial(_dispatch_kernel, tm=tm),
        grid=(t // tm,),
        in_specs=[pl.BlockSpec((1, 1, 2 * tm), lambda i: (i, 0, 0), memory_space=pltpu.SMEM),
                  pl.BlockSpec((tm, d), lambda i: (i, 0)),
                  pl.BlockSpec(memory_space=pl.ANY)],
        out_specs=pl.BlockSpec(memory_space=pl.ANY),
        out_shape=jax.ShapeDtypeStruct(xs_init.shape, xs_init.dtype),
        scratch_shapes=[pltpu.SemaphoreType.DMA(())],
        input_output_aliases={2: 0},
        compiler_params=_cparams(("arbitrary",)),
        name="dispatch",
    )(dest3, h, xs_init)


def _experts_kernel(te_ref, tv_ref, x_ref, wg_ref, wu_ref, wd_ref, o_ref, xb_ref, acc_ref):
    i = pl.program_id(0)
    f = pl.program_id(1)
    last = pl.num_programs(1) - 1
    valid = tv_ref[i] == 1

    @pl.when(valid)
    def _():
        @pl.when(f == 0)
        def _():
            xb_ref[...] = x_ref[...].astype(BF16)
            acc_ref[...] = jnp.zeros_like(acc_ref)

        xb = xb_ref[...]
        gate = _dot(xb, wg_ref[0])
        up = _dot(xb, wu_ref[0])
        acc_ref[...] += _dot((_silu(gate) * up).astype(BF16), wd_ref[0])

        @pl.when(f == last)
        def _():
            o_ref[...] = acc_ref[...]

    @pl.when(jnp.logical_and(jnp.logical_not(valid), f == last))
    def _():
        o_ref[...] = jnp.zeros_like(o_ref)


def experts(tile_e, tile_v, xs, wg, wu, wd, *, tm=1024, tf=512):
    p, d = xs.shape
    dff = wg.shape[2]
    nf = dff // tf

    def fsel(i, f, tv):
        return jnp.where(tv[i] == 1, f, nf - 1)

    grid_spec = pltpu.PrefetchScalarGridSpec(
        num_scalar_prefetch=2,
        grid=(p // tm, nf),
        in_specs=[pl.BlockSpec((tm, d), lambda i, f, te, tv: (i, 0)),
                  pl.BlockSpec((1, d, tf), lambda i, f, te, tv: (te[i], 0, fsel(i, f, tv))),
                  pl.BlockSpec((1, d, tf), lambda i, f, te, tv: (te[i], 0, fsel(i, f, tv))),
                  pl.BlockSpec((1, tf, d), lambda i, f, te, tv: (te[i], fsel(i, f, tv), 0))],
        out_specs=pl.BlockSpec((tm, d), lambda i, f, te, tv: (i, 0)),
        scratch_shapes=[pltpu.VMEM((tm, d), BF16), pltpu.VMEM((tm, d), F32)],
    )
    return pl.pallas_call(
        _experts_kernel,
        grid_spec=grid_spec,
        out_shape=jax.ShapeDtypeStruct((p, d), F32),
        compiler_params=_cparams(("arbitrary", "arbitrary")),
        name="experts",
    )(tile_e, tile_v, xs, wg, wu, wd)


def _combine_kernel(dest_ref, x_ref, meta_ref, g_ref, ys_ref, o_ref, buf_ref, sem, *, tm):
    def issue(r, carry):
        for k in range(2):
            d = dest_ref[0, 0, 2 * r + k]
            pltpu.make_async_copy(ys_ref.at[pl.ds(d, 1), :], buf_ref.at[k, pl.ds(r, 1), :], sem).start()
        return carry

    lax.fori_loop(0, tm, issue, 0)

    def drain(r, carry):
        for k in range(2):
            pltpu.make_async_copy(ys_ref.at[pl.ds(0, 1), :], buf_ref.at[0, pl.ds(0, 1), :], sem).wait()
        return carry

    lax.fori_loop(0, tm, drain, 0)
    meta = meta_ref[...]
    y = meta[:, 2:3] * buf_ref[0] + meta[:, 3:4] * buf_ref[1]
    o_ref[...] = x_ref[...] + g_ref[0] * y


def combine(dest3, x2, meta, g2, ys, *, seq, tm=256):
    t, d = x2.shape
    tiles_per_seq = seq // tm
    row = lambda i: (i, 0)
    return pl.pallas_call(
        functools.partial(_combine_kernel, tm=tm),
        grid=(t // tm,),
        in_specs=[pl.BlockSpec((1, 1, 2 * tm), lambda i: (i, 0, 0), memory_space=pltpu.SMEM),
                  pl.BlockSpec((tm, d), row),
                  pl.BlockSpec((tm, LANES), row),
                  pl.BlockSpec((1, 1, d), lambda i: (i // tiles_per_seq, 0, 0)),
                  pl.BlockSpec(memory_space=pl.ANY)],
        out_specs=pl.BlockSpec((tm, d), row),
        out_shape=jax.ShapeDtypeStruct((t, d), F32),
        scratch_shapes=[pltpu.VMEM((2, tm, d), F32), pltpu.SemaphoreType.DMA(())],
        compiler_params=_cparams(("arbitrary",)),
        name="combine",
    )(dest3, x2, meta, g2, ys)


def _final_kernel(x_ref, nw_ref, sh_ref, sc_ref, o_ref):
    o_ref[...] = _norm_mod(x_ref[...], nw_ref[...], sh_ref[0], sc_ref[0])


def final_norm(x2, nw, sh, sc, *, seq, tm=1024):
    t, d = x2.shape
    tiles_per_seq = seq // tm
    row = lambda i: (i, 0)
    bidx = lambda i: (i // tiles_per_seq, 0, 0)
    return pl.pallas_call(
        _final_kernel,
        grid=(t // tm,),
        in_specs=[pl.BlockSpec((tm, d), row),
                  pl.BlockSpec((1, d), lambda i: (0, 0)),
                  pl.BlockSpec((1, 1, d), bidx),
                  pl.BlockSpec((1, 1, d), bidx)],
        out_specs=pl.BlockSpec((tm, d), row),
        out_shape=jax.ShapeDtypeStruct((t, d), F32),
        compiler_params=_cparams(("arbitrary",)),
        name="final_norm",
    )(x2, nw, sh, sc)


def _moe_layer(x2, nw, sh, sc, g2, wr, wg, wu, wd, *, seq, tm_e):
    t, d = x2.shape
    wr_pad = jnp.zeros((d, LANES), F32).at[:, :N_EXPERTS].set(wr)
    h, meta, cnt = router(x2, nw, sh, sc, wr_pad, seq=seq)
    counts = cnt[0, :N_EXPERTS].astype(jnp.int32)
    tiles = (counts + tm_e - 1) // tm_e
    tile_end = jnp.cumsum(tiles)
    offs = (tile_end - tiles) * tm_e
    idx = meta[:, 0:2].astype(jnp.int32)
    rank = meta[:, 4:6].astype(jnp.int32)
    dest = offs[idx] + rank
    n_tiles = (2 * t) // tm_e + N_EXPERTS
    tid = jnp.arange(n_tiles, dtype=jnp.int32)
    tile_e = jnp.minimum(jnp.sum(tid[:, None] >= tile_end[None, :], axis=1), N_EXPERTS - 1)
    tile_v = (tid < tile_end[-1]).astype(jnp.int32)
    tile_e = jnp.where(tile_v == 1, tile_e, tile_e[jnp.maximum(tile_end[-1] - 1, 0)]).astype(jnp.int32)
    xs = dispatch(h, dest.reshape(t // 512, 1, 2 * 512), jnp.zeros((n_tiles * tm_e, d), F32))
    ys = experts(tile_e, tile_v, xs, wg, wu, wd, tm=tm_e)
    return combine(dest.reshape(t // 256, 1, 2 * 256), x2, meta, g2, ys, seq=seq)


def kernel(x, c, ada_w, ada_b, norm1_w, norm2_w, w_in, conv_w, a_log, dt_bias, dn_norm_w, gm_ln_w, gm_ln_b, gm_spatial_w, gm_spatial_b, w_out, ffn_w_gate, ffn_w_up, ffn_w_down, moe_router, moe_w_gate, moe_w_up, moe_w_down, final_ada_w, final_ada_b, final_norm_w):
    bsz, seq, d = x.shape
    depth = ada_w.shape[0]
    t = bsz * seq
    assert d == D_MODEL and seq % 1024 == 0
    x2 = x.reshape(t, d)

    mods = ada_mod(c, ada_w, ada_b[:, None, :])
    fmod = ada_mod(c, final_ada_w[None], final_ada_b[None, None, :])[0]

    w_in_p = jnp.concatenate(
        [w_in[:, :, 0:2048], w_in[:, :, 2056:3080], w_in[:, :, 2048:2056],
         jnp.zeros((depth, d, LANES - 8), F32)], axis=-1).astype(BF16)
    w_out_b = w_out.astype(BF16)
    causal = jnp.tril(jnp.ones((GM_CHUNK, GM_CHUNK), dtype=bool))
    ws = jnp.where(causal, gm_spatial_w, 0.0).astype(BF16)
    bs = jnp.broadcast_to(gm_spatial_b[..., None], (depth, GM_GROUPS, GM_CHUNK, LANES))
    hp = jnp.zeros((depth, 8, LANES), F32)
    hp = hp.at[:, 0, 4:8].set(-jnp.exp(a_log)).at[:, 1, 4:8].set(dt_bias)
    ffn_g, ffn_u, ffn_d = ffn_w_gate.astype(BF16), ffn_w_up.astype(BF16), ffn_w_down.astype(BF16)
    moe_g, moe_u, moe_d = moe_w_gate.astype(BF16), moe_w_up.astype(BF16), moe_w_down.astype(BF16)

    for i in range(depth):
        mod = mods[i].reshape(bsz, 6, 1, d)
        sh1, sc1, g1, sh2, sc2, g2 = (mod[:, k] for k in range(6))
        q, kb, vb, kt, zs, gc, gct, ogm = inproj(
            x2, norm1_w[i][None], sh1, sc1, w_in_p[i], conv_w[i], hp[i],
            gm_ln_w[i][None], gm_ln_b[i][None], ws[i], bs[i], seq=seq)
        odn = deltanet(q, kb, vb, kt, zs, gc, gct, dn_norm_w[i][None], batch=bsz, seq=seq)
        x2 = outproj(odn, ogm, w_out_b[i], x2, g1, seq=seq)
        j = i // 2
        if i % 2 == 0:
            x2 = ffn(x2, norm2_w[i][None], sh2, sc2, g2, ffn_g[j], ffn_u[j], ffn_d[j], seq=seq)
        else:
            x2 = _moe_layer(x2, norm2_w[i][None], sh2, sc2, g2, moe_router[j],
                            moe_g[j], moe_u[j], moe_d[j], seq=seq, tm_e=1024)
    fm = fmod.reshape(bsz, 2, 1, d)
    out = final_norm(x2, final_norm_w[None], fm[:, 0], fm[:, 1], seq=seq)
    return out.reshape(bsz, seq, d)
```

```python
import functools
import math

import jax
import jax.numpy as jnp
from jax import lax
from jax.experimental import pallas as pl
from jax.experimental.pallas import tpu as pltpu

F32 = jnp.float32
BF16 = jnp.bfloat16

EPS = 1e-6
D_MODEL = 1024
DN_HEADS = 4
HEAD_DIM = 128
DN_WIDTH = DN_HEADS * HEAD_DIM
CONV_WIDTH = 4
GM_GROUPS = 4
GM_WIDTH = 512
GM_CHUNK = 128
DN_CHUNK = 128
N_EXPERTS = 8
LANES = 128
C_QKV, C_Z, C_GU, C_GV, C_BA, C_END = 0, 1536, 2048, 2560, 3072, 3200

VMEM_LIMIT = 56 * 1024 * 1024


def _cparams(sem):
    return pltpu.CompilerParams(dimension_semantics=sem, vmem_limit_bytes=VMEM_LIMIT)


def _sigmoid(x):
    return 0.5 + 0.5 * jnp.tanh(0.5 * x)


def _silu(x):
    return x * _sigmoid(x)


def _gelu_tanh(x):
    return 0.5 * x * (1.0 + jnp.tanh(math.sqrt(2.0 / math.pi) * (x + 0.044715 * (x * x * x))))


def _softplus(x):
    return jnp.maximum(x, 0.0) + jnp.log(1.0 + jnp.exp(-jnp.abs(x)))


def _split_hi_lo(x):
    hi = x.astype(BF16)
    lo = (x - hi.astype(F32)).astype(BF16)
    return hi, lo


def _dot(a, b):
    return jnp.dot(a, b, preferred_element_type=F32)


def _ada_kernel(c_ref, w_ref, b_ref, o_ref):
    c = c_ref[...]
    ca = _silu(c)
    c_hi, c_lo = _split_hi_lo(ca)
    w = w_ref[0]
    w_hi, w_lo = _split_hi_lo(w)
    o_ref[0] = _dot(c_hi, w_hi) + _dot(c_lo, w_hi) + _dot(c_hi, w_lo) + b_ref[0]


def ada_mod(c, w, b, tn=2048):
    nl, d, n = w.shape
    bsz = c.shape[0]
    return pl.pallas_call(
        _ada_kernel,
        grid=(nl, n // tn),
        in_specs=[pl.BlockSpec((bsz, d), lambda l, j: (0, 0)),
                  pl.BlockSpec((1, d, tn), lambda l, j: (l, 0, j)),
                  pl.BlockSpec((1, 1, tn), lambda l, j: (l, 0, j))],
        out_specs=pl.BlockSpec((1, bsz, tn), lambda l, j: (l, 0, j)),
        out_shape=jax.ShapeDtypeStruct((nl, bsz, n), F32),
        compiler_params=_cparams(("arbitrary", "arbitrary")),
        name="ada_mod",
    )(c, w, b)


def _norm_mod(x, nw, sh, sc):
    ms = jnp.mean(x * x, axis=-1, keepdims=True)
    return (x * lax.rsqrt(ms + EPS) * nw) * (1.0 + sc) + sh


def _inproj_kernel(x_ref, nw_ref, sh_ref, sc_ref, w_ref, cw_ref, hp_ref, lnw_ref, lnb_ref,
                   ws_ref, bs_ref,
                   q_ref, kb_ref, vb_ref, kt_ref, zs_ref, gc_ref, gct_ref, ogm_ref,
                   halo_ref, *, tm, tiles_per_seq):
    i = pl.program_id(0)
    x = x_ref[...]
    hb = _norm_mod(x, nw_ref[...], sh_ref[0], sc_ref[0]).astype(BF16)

    pre = _dot(hb, w_ref[:, C_QKV:C_Z])
    @pl.when((i % tiles_per_seq) == 0)
    def _():
        halo_ref[...] = jnp.zeros_like(halo_ref)

    halo = halo_ref[...]
    halo_ref[...] = pre[tm - 8:tm, :]
    cw = cw_ref[...]
    row8 = lax.broadcasted_iota(jnp.int32, (8, pre.shape[1]), 0)
    conv = pre * cw[CONV_WIDTH - 1:CONV_WIDTH, :]
    for s in range(1, CONV_WIDTH):
        sh = pltpu.roll(pre, s, 0)
        fix = pltpu.roll(halo, s, 0)
        top = jnp.where(row8 < s, fix, sh[0:8, :])
        sh = jnp.concatenate([top, sh[8:, :]], axis=0)
        conv = conv + sh * cw[CONV_WIDTH - 1 - s:CONV_WIDTH - s, :]
    act = _silu(conv)

    ba = _dot(hb, w_ref[:, C_BA:C_END])
    beta = _sigmoid(ba)
    hp = hp_ref[...]
    g = hp[0:1, :] * _softplus(ba + hp[1:2, :])
    rowc = lax.broadcasted_iota(jnp.int32, (tm, LANES), 0) % DN_CHUNK
    gc = g
    s = 1
    while s < DN_CHUNK:
        gc = gc + jnp.where(rowc >= s, pltpu.roll(gc, s, 0), 0.0)
        s *= 2
    gc_ref[...] = gc
    gct = gc.T
    for j in range(tm // DN_CHUNK):
        gct_ref[j] = gct[0:8, j * DN_CHUNK:(j + 1) * DN_CHUNK]

    kn_all = []
    for h in range(DN_HEADS):
        lo, hi = h * HEAD_DIM, (h + 1) * HEAD_DIM
        qh = act[:, lo:hi]
        kh = act[:, DN_WIDTH + lo:DN_WIDTH + hi]
        vh = act[:, 2 * DN_WIDTH + lo:2 * DN_WIDTH + hi]
        qn = qh * lax.rsqrt(jnp.sum(qh * qh, axis=-1, keepdims=True) + EPS) * (HEAD_DIM ** -0.5)
        kn = kh * lax.rsqrt(jnp.sum(kh * kh, axis=-1, keepdims=True) + EPS)
        bh = beta[:, h:h + 1]
        q_ref[:, lo:hi] = qn.astype(BF16)
        kb_ref[:, lo:hi] = (kn * bh).astype(BF16)
        vb_ref[:, lo:hi] = (vh * bh).astype(BF16)
        kn_all.append(kn)
    kt = jnp.concatenate(kn_all, axis=1).T
    for j in range(tm // DN_CHUNK):
        kt_ref[j] = kt[:, j * DN_CHUNK:(j + 1) * DN_CHUNK].astype(BF16)

    zs_ref[...] = _silu(_dot(hb, w_ref[:, C_Z:C_GU])).astype(BF16)

    gu = _gelu_tanh(_dot(hb, w_ref[:, C_GU:C_GV]))
    gv = _gelu_tanh(_dot(hb, w_ref[:, C_GV:C_BA]))
    mu = jnp.mean(gv, axis=-1, keepdims=True)
    gvc = gv - mu
    var = jnp.mean(gvc * gvc, axis=-1, keepdims=True)
    gvn = (gvc * lax.rsqrt(var + EPS) * lnw_ref[...] + lnb_ref[...]).astype(BF16)
    for j in range(tm // GM_CHUNK):
        r0, r1 = j * GM_CHUNK, (j + 1) * GM_CHUNK
        for gi in range(GM_GROUPS):
            c0, c1 = gi * LANES, (gi + 1) * LANES
            sp = _dot(ws_ref[gi], gvn[r0:r1, c0:c1]) + bs_ref[gi]
            ogm_ref[r0:r1, c0:c1] = (gu[r0:r1, c0:c1] * sp).astype(BF16)


def inproj(x2, nw, sh, sc, w, cw, hp, lnw, lnb, ws, bs, *, seq, tm=512):
    t, d = x2.shape
    nt = t // tm
    tiles_per_seq = seq // tm
    nc = tm // DN_CHUNK
    row = lambda i: (i, 0)
    const2 = lambda i: (0, 0)
    const3 = lambda i: (0, 0, 0)
    bidx = lambda i: (i // tiles_per_seq, 0, 0)
    out_shape = [
        jax.ShapeDtypeStruct((t, DN_WIDTH), BF16),
        jax.ShapeDtypeStruct((t, DN_WIDTH), BF16),
        jax.ShapeDtypeStruct((t, DN_WIDTH), BF16),
        jax.ShapeDtypeStruct((t // DN_CHUNK, DN_WIDTH, DN_CHUNK), BF16),
        jax.ShapeDtypeStruct((t, DN_WIDTH), BF16),
        jax.ShapeDtypeStruct((t, LANES), F32),
        jax.ShapeDtypeStruct((t // DN_CHUNK, 8, DN_CHUNK), F32),
        jax.ShapeDtypeStruct((t, GM_WIDTH), BF16),
    ]
    out_specs = [
        pl.BlockSpec((tm, DN_WIDTH), row),
        pl.BlockSpec((tm, DN_WIDTH), row),
        pl.BlockSpec((tm, DN_WIDTH), row),
        pl.BlockSpec((nc, DN_WIDTH, DN_CHUNK), lambda i: (i, 0, 0)),
        pl.BlockSpec((tm, DN_WIDTH), row),
        pl.BlockSpec((tm, LANES), row),
        pl.BlockSpec((nc, 8, DN_CHUNK), lambda i: (i, 0, 0)),
        pl.BlockSpec((tm, GM_WIDTH), row),
    ]
    in_specs = [
        pl.BlockSpec((tm, d), row),
        pl.BlockSpec((1, d), const2),
        pl.BlockSpec((1, 1, d), bidx),
        pl.BlockSpec((1, 1, d), bidx),
        pl.BlockSpec((d, C_END), const2),
        pl.BlockSpec((CONV_WIDTH, 3 * DN_WIDTH), const2),
        pl.BlockSpec((8, LANES), const2),
        pl.BlockSpec((1, GM_WIDTH), const2),
        pl.BlockSpec((1, GM_WIDTH), const2),
        pl.BlockSpec((GM_GROUPS, GM_CHUNK, GM_CHUNK), const3),
        pl.BlockSpec((GM_GROUPS, GM_CHUNK, LANES), const3),
    ]
    return pl.pallas_call(
        functools.partial(_inproj_kernel, tm=tm, tiles_per_seq=tiles_per_seq),
        grid=(nt,),
        in_specs=in_specs,
        out_specs=out_specs,
        out_shape=out_shape,
        scratch_shapes=[pltpu.VMEM((8, 3 * DN_WIDTH), F32)],
        compiler_params=_cparams(("arbitrary",)),
        name="inproj",
    )(x2, nw, sh, sc, w, cw, hp, lnw, lnb, ws, bs)


INV_BASE = 8


def _unit_lower_inverse(a_list, ri, ci):
    c = a_list[0].shape[0]
    same = (ri // INV_BASE) == (ci // INV_BASE)
    eye = jnp.where(ri == ci, 1.0, 0.0)
    ad = [jnp.where(same, a, 0.0) for a in a_list]
    x = [eye - a for a in ad]
    pb = [a.astype(BF16) for a in ad]
    p = [_dot(b, b) for b in pb]
    n = 2
    while True:
        pb = [m.astype(BF16) for m in p]
        x = [xi + _dot(xi.astype(BF16), b) for xi, b in zip(x, pb)]
        n *= 2
        if n >= INV_BASE:
            break
        p = [_dot(b, b) for b in pb]
    s = INV_BASE
    while s < c:
        off = jnp.logical_and((ri // (2 * s)) == (ci // (2 * s)), (ri // s) != (ci // s))
        xb = [xi.astype(BF16) for xi in x]
        y = [_dot(b, jnp.where(off, a, 0.0).astype(BF16)).astype(BF16) for b, a in zip(xb, a_list)]
        x = [xi - _dot(yi, b) for xi, yi, b in zip(x, y, xb)]
        s *= 2
    return x


def _deltanet_kernel(q_ref, kb_ref, vb_ref, kt_ref, zs_ref, gc_ref, gct_ref, nw_ref,
                     o_ref, s_ref, u_ref, w_ref, attn_ref, *, tl, unroll):
    l = pl.program_id(1)

    @pl.when(l == 0)
    def _():
        s_ref[...] = jnp.zeros_like(s_ref)

    c = DN_CHUNK
    ri = lax.broadcasted_iota(jnp.int32, (c, c), 0)
    ci = lax.broadcasted_iota(jnp.int32, (c, c), 1)
    tri = ri >= ci
    strict = ri > ci
    nw = nw_ref[...]
    heads = [(h, h * HEAD_DIM, (h + 1) * HEAD_DIM) for h in range(DN_HEADS)]

    def prep(it, carry):
        items = []
        for sub in range(unroll):
            ch = it * unroll + sub
            r0 = pl.multiple_of(ch * c, c)
            gc = gc_ref[pl.ds(r0, c), :]
            gct = gct_ref[ch]
            for h, lo, hi in heads:
                items.append((ch, r0, h, lo, hi,
                              jnp.broadcast_to(gc[:, 4 + h:5 + h], (c, c)),
                              jnp.broadcast_to(gct[4 + h:5 + h, :], (c, c))))
        kts = [kt_ref[ch, lo:hi, :] for ch, r0, h, lo, hi, _, _ in items]
        kbs = [kb_ref[pl.ds(r0, c), lo:hi] for ch, r0, h, lo, hi, _, _ in items]
        decay = [jnp.where(tri, jnp.exp(jnp.minimum(gcol - grow, 0.0)), 0.0)
                 for *_, gcol, grow in items]
        a = [jnp.where(strict, _dot(kb, kt) * dc, 0.0) for kb, kt, dc in zip(kbs, kts, decay)]
        for (ch, r0, h, lo, hi, _, _), kt, dc in zip(items, kts, decay):
            attn_ref[ch, h] = (_dot(q_ref[pl.ds(r0, c), lo:hi], kt) * dc).astype(BF16)
        tb = [t.astype(BF16) for t in _unit_lower_inverse(a, ri, ci)]
        for (ch, r0, h, lo, hi, gcol, _), t, kb in zip(items, tb, kbs):
            u_ref[pl.ds(r0, c), lo:hi] = _dot(t, vb_ref[pl.ds(r0, c), lo:hi])
            w_ref[pl.ds(r0, c), lo:hi] = _dot(
                t, (kb.astype(F32) * jnp.exp(gcol)).astype(BF16)).astype(BF16)
        return carry

    lax.fori_loop(0, tl // (c * unroll), prep, 0)

    def step(ch, carry):
        r0 = pl.multiple_of(ch * c, c)
        gc = gc_ref[pl.ds(r0, c), :]
        gcol = [jnp.broadcast_to(gc[:, 4 + h:5 + h], (c, c)) for h, lo, hi in heads]
        glast = [jnp.broadcast_to(g[c - 1:c, :], (c, c)) for g in gcol]
        q = [q_ref[pl.ds(r0, c), lo:hi] for h, lo, hi in heads]
        sb = [s_ref[h].astype(BF16) for h, lo, hi in heads]
        ws = [_dot(w_ref[pl.ds(r0, c), lo:hi], s) for (h, lo, hi), s in zip(heads, sb)]
        qs = [_dot(qh, s) for qh, s in zip(q, sb)]
        v_new = [u_ref[pl.ds(r0, c), lo:hi] - w for (h, lo, hi), w in zip(heads, ws)]
        av = [_dot(attn_ref[ch, h], v.astype(BF16)) for (h, lo, hi), v in zip(heads, v_new)]
        kv = [_dot(kt_ref[ch, lo:hi, :], (v * jnp.exp(gl - g)).astype(BF16))
              for (h, lo, hi), v, gl, g in zip(heads, v_new, glast, gcol)]
        for (h, lo, hi), upd, gl in zip(heads, kv, glast):
            s_ref[h] = s_ref[h] * jnp.exp(gl) + upd
        for (h, lo, hi), qsh, avh, g in zip(heads, qs, av, gcol):
            o = qsh * jnp.exp(g) + avh
            ms = jnp.mean(o * o, axis=-1, keepdims=True)
            on = o * lax.rsqrt(ms + EPS) * nw
            zs = zs_ref[pl.ds(r0, c), lo:hi].astype(F32)
            o_ref[pl.ds(r0, c), lo:hi] = (on * zs).astype(BF16)
        return carry

    lax.fori_loop(0, tl // c, step, 0)


def deltanet(q, kb, vb, kt, zs, gc, gct, nw, *, batch, seq, tl=512):
    t = q.shape[0]
    nl = seq // tl
    nc = tl // DN_CHUNK
    row = lambda b, l: (b * nl + l, 0)
    row3 = lambda b, l: (b * nl + l, 0, 0)
    return pl.pallas_call(
        functools.partial(_deltanet_kernel, tl=tl, unroll=2),
        grid=(batch, nl),
        in_specs=[pl.BlockSpec((tl, DN_WIDTH), row),
                  pl.BlockSpec((tl, DN_WIDTH), row),
                  pl.BlockSpec((tl, DN_WIDTH), row),
                  pl.BlockSpec((nc, DN_WIDTH, DN_CHUNK), row3),
                  pl.BlockSpec((tl, DN_WIDTH), row),
                  pl.BlockSpec((tl, LANES), row),
                  pl.BlockSpec((nc, 8, DN_CHUNK), row3),
                  pl.BlockSpec((1, HEAD_DIM), lambda b, l: (0, 0))],
        out_specs=pl.BlockSpec((tl, DN_WIDTH), row),
        out_shape=jax.ShapeDtypeStruct((t, DN_WIDTH), BF16),
        scratch_shapes=[pltpu.VMEM((DN_HEADS, HEAD_DIM, HEAD_DIM), F32),
                        pltpu.VMEM((tl, DN_WIDTH), F32),
                        pltpu.VMEM((tl, DN_WIDTH), BF16),
                        pltpu.VMEM((nc, DN_HEADS, DN_CHUNK, DN_CHUNK), BF16)],
        compiler_params=_cparams(("arbitrary", "arbitrary")),
        name="deltanet",
    )(q, kb, vb, kt, zs, gc, gct, nw)


def _outproj_kernel(odn_ref, ogm_ref, w_ref, x_ref, g_ref, o_ref):
    mix = _dot(odn_ref[...], w_ref[0:DN_WIDTH, :]) + _dot(ogm_ref[...], w_ref[DN_WIDTH:, :])
    o_ref[...] = x_ref[...] + g_ref[0] * mix


def outproj(odn, ogm, w, x2, g1, *, seq, tm=1024):
    t, d = x2.shape
    tiles_per_seq = seq // tm
    row = lambda i: (i, 0)
    return pl.pallas_call(
        _outproj_kernel,
        grid=(t // tm,),
        in_specs=[pl.BlockSpec((tm, DN_WIDTH), row),
                  pl.BlockSpec((tm, GM_WIDTH), row),
                  pl.BlockSpec((d, d), lambda i: (0, 0)),
                  pl.BlockSpec((tm, d), row),
                  pl.BlockSpec((1, 1, d), lambda i: (i // tiles_per_seq, 0, 0))],
        out_specs=pl.BlockSpec((tm, d), row),
        out_shape=jax.ShapeDtypeStruct((t, d), F32),
        compiler_params=_cparams(("arbitrary",)),
        name="outproj",
    )(odn, ogm, w, x2, g1)


def _ffn_kernel(x_ref, nw_ref, sh_ref, sc_ref, g_ref, wg_ref, wu_ref, wd_ref, o_ref,
                hb_ref, acc_ref):
    f = pl.program_id(1)

    @pl.when(f == 0)
    def _():
        hb_ref[...] = _norm_mod(x_ref[...], nw_ref[...], sh_ref[0], sc_ref[0]).astype(BF16)
        acc_ref[...] = jnp.zeros_like(acc_ref)

    hb = hb_ref[...]
    gate = _dot(hb, wg_ref[...])
    up = _dot(hb, wu_ref[...])
    acc_ref[...] += _dot((_silu(gate) * up).astype(BF16), wd_ref[...])

    @pl.when(f == pl.num_programs(1) - 1)
    def _():
        o_ref[...] = x_ref[...] + g_ref[0] * acc_ref[...]


def ffn(x2, nw, sh, sc, g2, wg, wu, wd, *, seq, tm=512, tf=1408):
    t, d = x2.shape
    dff = wg.shape[1]
    tiles_per_seq = seq // tm
    row = lambda i, f: (i, 0)
    bidx = lambda i, f: (i // tiles_per_seq, 0, 0)
    return pl.pallas_call(
        _ffn_kernel,
        grid=(t // tm, dff // tf),
        in_specs=[pl.BlockSpec((tm, d), row),
                  pl.BlockSpec((1, d), lambda i, f: (0, 0)),
                  pl.BlockSpec((1, 1, d), bidx),
                  pl.BlockSpec((1, 1, d), bidx),
                  pl.BlockSpec((1, 1, d), bidx),
                  pl.BlockSpec((d, tf), lambda i, f: (0, f)),
                  pl.BlockSpec((d, tf), lambda i, f: (0, f)),
                  pl.BlockSpec((tf, d), lambda i, f: (f, 0))],
        out_specs=pl.BlockSpec((tm, d), row),
        out_shape=jax.ShapeDtypeStruct((t, d), F32),
        scratch_shapes=[pltpu.VMEM((tm, d), BF16), pltpu.VMEM((tm, d), F32)],
        compiler_params=_cparams(("arbitrary", "arbitrary")),
        name="ffn",
    )(x2, nw, sh, sc, g2, wg, wu, wd)


def _router_kernel(x_ref, nw_ref, sh_ref, sc_ref, wr_ref, h_ref, meta_ref, cnt_ref,
                   carry_ref, *, tm):
    i = pl.program_id(0)

    @pl.when(i == 0)
    def _():
        carry_ref[...] = jnp.zeros_like(carry_ref)

    h = _norm_mod(x_ref[...], nw_ref[...], sh_ref[0], sc_ref[0])
    h_ref[...] = h
    h_hi, h_lo = _split_hi_lo(h)
    w_hi, w_lo = _split_hi_lo(wr_ref[...])
    logits = _dot(h_hi, w_hi) + _dot(h_lo, w_hi) + _dot(h_hi, w_lo)
    lane = lax.broadcasted_iota(jnp.int32, (tm, LANES), 1).astype(F32)
    neg = jnp.float32(-jnp.inf)
    lg = jnp.where(lane < N_EXPERTS, logits, neg)
    m1 = jnp.max(lg, axis=-1, keepdims=True)
    i1 = jnp.min(jnp.where(lg == m1, lane, float(LANES)), axis=-1, keepdims=True)
    lg2 = jnp.where(lane == i1, neg, lg)
    m2 = jnp.max(lg2, axis=-1, keepdims=True)
    i2 = jnp.min(jnp.where(lg2 == m2, lane, float(LANES)), axis=-1, keepdims=True)
    e2 = jnp.exp(m2 - m1)
    w1 = 1.0 / (1.0 + e2)
    w2 = e2 / (1.0 + e2)
    oh1 = lane == i1
    oh2 = lane == i2
    onehot = jnp.where(oh1 | oh2, 1.0, 0.0)
    ri = lax.broadcasted_iota(jnp.int32, (tm, tm), 0)
    ci = lax.broadcasted_iota(jnp.int32, (tm, tm), 1)
    lower = jnp.where(ri > ci, 1.0, 0.0).astype(BF16)
    before = _dot(lower, onehot.astype(BF16)) + carry_ref[0:1, :]
    r1 = jnp.sum(jnp.where(oh1, before, 0.0), axis=-1, keepdims=True)
    r2 = jnp.sum(jnp.where(oh2, before, 0.0), axis=-1, keepdims=True)
    total = carry_ref[0:1, :] + jnp.sum(onehot, axis=0, keepdims=True)
    carry_ref[...] = jnp.broadcast_to(total, carry_ref.shape)
    cnt_ref[...] = jnp.broadcast_to(total, cnt_ref.shape)
    meta = jnp.where(lane == 0, i1, 0.0)
    meta = jnp.where(lane == 1, i2, meta)
    meta = jnp.where(lane == 2, w1, meta)
    meta = jnp.where(lane == 3, w2, meta)
    meta = jnp.where(lane == 4, r1, meta)
    meta = jnp.where(lane == 5, r2, meta)
    meta_ref[...] = meta


def router(x2, nw, sh, sc, wr, *, seq, tm=512):
    t, d = x2.shape
    tiles_per_seq = seq // tm
    row = lambda i: (i, 0)
    bidx = lambda i: (i // tiles_per_seq, 0, 0)
    return pl.pallas_call(
        functools.partial(_router_kernel, tm=tm),
        grid=(t // tm,),
        in_specs=[pl.BlockSpec((tm, d), row),
                  pl.BlockSpec((1, d), lambda i: (0, 0)),
                  pl.BlockSpec((1, 1, d), bidx),
                  pl.BlockSpec((1, 1, d), bidx),
                  pl.BlockSpec((d, LANES), lambda i: (0, 0))],
        out_specs=[pl.BlockSpec((tm, d), row),
                   pl.BlockSpec((tm, LANES), row),
                   pl.BlockSpec((8, LANES), lambda i: (0, 0))],
        out_shape=[jax.ShapeDtypeStruct((t, d), F32),
                   jax.ShapeDtypeStruct((t, LANES), F32),
                   jax.ShapeDtypeStruct((8, LANES), F32)],
        scratch_shapes=[pltpu.VMEM((8, LANES), F32)],
        compiler_params=_cparams(("arbitrary",)),
        name="router",
    )(x2, nw, sh, sc, wr)


def _experts_kernel(te_ref, tv_ref, src_cur, src_next, dst_prev, dst_cur, h_hbm,
                    wg_ref, wu_ref, wd_ref, ys_hbm,
                    xbuf, xb_ref, acc_ref, obuf, gsem, ssem, *, tm, rows):
    i = pl.program_id(0)
    f = pl.program_id(1)
    last_i = pl.num_programs(0) - 1
    last_f = pl.num_programs(1) - 1
    valid = tv_ref[i] == 1

    def gather_row(idx_ref, r):
        t = idx_ref[0, 0, r]
        pltpu.make_async_copy(h_hbm.at[pl.ds(t, 1), :], xbuf.at[pl.ds(r, 1), :], gsem).start()

    def scatter_row(idx_ref, r):
        d = idx_ref[0, 0, r]
        pltpu.make_async_copy(obuf.at[pl.ds(r, 1), :], ys_hbm.at[pl.ds(d, 1), :], ssem).start()

    def wait_gather():
        pltpu.make_async_copy(h_hbm.at[pl.ds(0, tm), :], xbuf, gsem).wait()

    def wait_scatter():
        pltpu.make_async_copy(obuf, ys_hbm.at[pl.ds(0, tm), :], ssem).wait()

    @pl.when(f == 0)
    def _():
        @pl.when(i == 0)
        def _():
            obuf[...] = jnp.zeros_like(obuf)

            def first(r, carry):
                gather_row(src_cur, r)
                return carry

            lax.fori_loop(0, tm, first, 0)

        wait_gather()
        xb_ref[...] = xbuf[...].astype(BF16)
        acc_ref[...] = jnp.zeros_like(acc_ref)

    def move_rows():
        base = f * rows
        for j in range(rows):
            gather_row(src_next, base + j)
            scatter_row(dst_prev, base + j)

    @pl.when(valid)
    def _():
        move_rows()
        xb = xb_ref[...]
        gate = _dot(xb, wg_ref[0])
        up = _dot(xb, wu_ref[0])
        acc_ref[...] += _dot((_silu(gate) * up).astype(BF16), wd_ref[0])

    @pl.when(jnp.logical_not(valid))
    def _():
        move_rows()

    @pl.when(f == last_f)
    def _():
        wait_scatter()
        obuf[...] = acc_ref[...]

        @pl.when(i == last_i)
        def _():
            def final(r, carry):
                scatter_row(dst_cur, r)
                return carry

            lax.fori_loop(0, tm, final, 0)
            wait_scatter()
            wait_gather()


def experts(tile_e, tile_v, src3, dst3, h, wg, wu, wd, *, tm, tf=512):
    nt = src3.shape[0]
    d = h.shape[1]
    dff = wg.shape[2]
    nf = dff // tf
    rows = tm // nf
    assert rows * nf == tm

    def fsel(i, f, tv):
        return jnp.where(tv[i] == 1, f, nf - 1)

    smem = functools.partial(pl.BlockSpec, (1, 1, tm), memory_space=pltpu.SMEM)
    grid_spec = pltpu.PrefetchScalarGridSpec(
        num_scalar_prefetch=2,
        grid=(nt, nf),
        in_specs=[smem(lambda i, f, te, tv: (i, 0, 0)),
                  smem(lambda i, f, te, tv: (jnp.minimum(i + 1, nt - 1), 0, 0)),
                  smem(lambda i, f, te, tv: (jnp.maximum(i - 1, 0), 0, 0)),
                  smem(lambda i, f, te, tv: (i, 0, 0)),
                  pl.BlockSpec(memory_space=pl.ANY),
                  pl.BlockSpec((1, d, tf), lambda i, f, te, tv: (te[i], 0, fsel(i, f, tv))),
                  pl.BlockSpec((1, d, tf), lambda i, f, te, tv: (te[i], 0, fsel(i, f, tv))),
                  pl.BlockSpec((1, tf, d), lambda i, f, te, tv: (te[i], fsel(i, f, tv), 0))],
        out_specs=pl.BlockSpec(memory_space=pl.ANY),
        scratch_shapes=[pltpu.VMEM((tm, d), F32), pltpu.VMEM((tm, d), BF16),
                        pltpu.VMEM((tm, d), F32), pltpu.VMEM((tm, d), F32),
                        pltpu.SemaphoreType.DMA(()), pltpu.SemaphoreType.DMA(())],
    )
    return pl.pallas_call(
        functools.partial(_experts_kernel, tm=tm, rows=rows),
        grid_spec=grid_spec,
        out_shape=jax.ShapeDtypeStruct((nt * tm, d), F32),
        compiler_params=_cparams(("arbitrary", "arbitrary")),
        name="experts",
    )(tile_e, tile_v, src3, src3, dst3, dst3, h, wg, wu, wd)


def _combine_kernel(x_ref, meta_ref, g_ref, y0_ref, y1_ref, o_ref):
    meta = meta_ref[...]
    y = meta[:, 2:3] * y0_ref[...] + meta[:, 3:4] * y1_ref[...]
    o_ref[...] = x_ref[...] + g_ref[0] * y


def combine(x2, meta, g2, ys, *, seq, tm=512):
    t, d = x2.shape
    tiles_per_seq = seq // tm
    nt = t // tm
    row = lambda i: (i, 0)
    return pl.pallas_call(
        _combine_kernel,
        grid=(nt,),
        in_specs=[pl.BlockSpec((tm, d), row),
                  pl.BlockSpec((tm, LANES), row),
                  pl.BlockSpec((1, 1, d), lambda i: (i // tiles_per_seq, 0, 0)),
                  pl.BlockSpec((tm, d), row),
                  pl.BlockSpec((tm, d), lambda i: (nt + i, 0))],
        out_specs=pl.BlockSpec((tm, d), row),
        out_shape=jax.ShapeDtypeStruct((t, d), F32),
        compiler_params=_cparams(("arbitrary",)),
        name="combine",
    )(x2, meta, g2, ys, ys)


def _final_kernel(x_ref, nw_ref, sh_ref, sc_ref, o_ref):
    o_ref[...] = _norm_mod(x_ref[...], nw_ref[...], sh_ref[0], sc_ref[0])


def final_norm(x2, nw, sh, sc, *, seq, tm=1024):
    t, d = x2.shape
    tiles_per_seq = seq // tm
    row = lambda i: (i, 0)
    bidx = lambda i: (i // tiles_per_seq, 0, 0)
    return pl.pallas_call(
        _final_kernel,
        grid=(t // tm,),
        in_specs=[pl.BlockSpec((tm, d), row),
                  pl.BlockSpec((1, d), lambda i: (0, 0)),
                  pl.BlockSpec((1, 1, d), bidx),
                  pl.BlockSpec((1, 1, d), bidx)],
        out_specs=pl.BlockSpec((tm, d), row),
        out_shape=jax.ShapeDtypeStruct((t, d), F32),
        compiler_params=_cparams(("arbitrary",)),
        name="final_norm",
    )(x2, nw, sh, sc)


def _moe_layer(x2, nw, sh, sc, g2, wr, wg, wu, wd, *, seq, tm_e):
    t, d = x2.shape
    wr_pad = jnp.zeros((d, LANES), F32).at[:, :N_EXPERTS].set(wr)
    h, meta, cnt = router(x2, nw, sh, sc, wr_pad, seq=seq)
    counts = cnt[0, :N_EXPERTS].astype(jnp.int32)
    tiles = (counts + tm_e - 1) // tm_e
    tile_end = jnp.cumsum(tiles)
    offs = (tile_end - tiles) * tm_e
    idx = meta[:, 0:2].astype(jnp.int32)
    rank = meta[:, 4:6].astype(jnp.int32)
    dest = (offs[idx] + rank).reshape(-1)
    n_tiles = -(-(2 * t) // tm_e) + N_EXPERTS
    p = n_tiles * tm_e
    tid = jnp.arange(n_tiles, dtype=jnp.int32)
    tile_e = jnp.minimum(jnp.sum(tid[:, None] >= tile_end[None, :], axis=1), N_EXPERTS - 1)
    tile_v = (tid < tile_end[-1]).astype(jnp.int32)
    tile_e = jnp.where(tile_v == 1, tile_e, tile_e[jnp.maximum(tile_end[-1] - 1, 0)]).astype(jnp.int32)
    entry = jnp.zeros((p,), jnp.int32).at[dest].set(jnp.arange(1, 2 * t + 1, dtype=jnp.int32)) - 1
    is_pad = entry < 0
    pad_rank = jnp.cumsum(is_pad.astype(jnp.int32)) - 1
    src = jnp.where(is_pad, 0, entry // 2)
    dst = jnp.where(is_pad, 2 * t + pad_rank, (entry % 2) * t + entry // 2)
    ys = experts(tile_e, tile_v, src.reshape(n_tiles, 1, tm_e), dst.reshape(n_tiles, 1, tm_e),
                 h, wg, wu, wd, tm=tm_e)
    return combine(x2, meta, g2, ys, seq=seq)


def kernel(x, c, ada_w, ada_b, norm1_w, norm2_w, w_in, conv_w, a_log, dt_bias, dn_norm_w, gm_ln_w, gm_ln_b, gm_spatial_w, gm_spatial_b, w_out, ffn_w_gate, ffn_w_up, ffn_w_down, moe_router, moe_w_gate, moe_w_up, moe_w_down, final_ada_w, final_ada_b, final_norm_w):
    bsz, seq, d = x.shape
    depth = ada_w.shape[0]
    t = bsz * seq
    assert d == D_MODEL and seq % 1024 == 0
    x2 = x.reshape(t, d)

    mods = ada_mod(c, ada_w, ada_b[:, None, :])
    fmod = ada_mod(c, final_ada_w[None], final_ada_b[None, None, :])[0]

    w_in_p = jnp.concatenate(
        [w_in[:, :, 0:2048], w_in[:, :, 2056:3080], w_in[:, :, 2048:2056],
         jnp.zeros((depth, d, LANES - 8), F32)], axis=-1).astype(BF16)
    w_out_b = w_out.astype(BF16)
    causal = jnp.tril(jnp.ones((GM_CHUNK, GM_CHUNK), dtype=bool))
    ws = jnp.where(causal, gm_spatial_w, 0.0).astype(BF16)
    bs = jnp.broadcast_to(gm_spatial_b[..., None], (depth, GM_GROUPS, GM_CHUNK, LANES))
    hp = jnp.zeros((depth, 8, LANES), F32)
    hp = hp.at[:, 0, 4:8].set(-jnp.exp(a_log)).at[:, 1, 4:8].set(dt_bias)
    ffn_g, ffn_u, ffn_d = ffn_w_gate.astype(BF16), ffn_w_up.astype(BF16), ffn_w_down.astype(BF16)
    moe_g, moe_u, moe_d = moe_w_gate.astype(BF16), moe_w_up.astype(BF16), moe_w_down.astype(BF16)

    for i in range(depth):
        mod = mods[i].reshape(bsz, 6, 1, d)
        sh1, sc1, g1, sh2, sc2, g2 = (mod[:, k] for k in range(6))
        q, kb, vb, kt, zs, gc, gct, ogm = inproj(
            x2, norm1_w[i][None], sh1, sc1, w_in_p[i], conv_w[i], hp[i],
            gm_ln_w[i][None], gm_ln_b[i][None], ws[i], bs[i], seq=seq)
        odn = deltanet(q, kb, vb, kt, zs, gc, gct, dn_norm_w[i][None], batch=bsz, seq=seq)
        x2 = outproj(odn, ogm, w_out_b[i], x2, g1, seq=seq)
        j = i // 2
        if i % 2 == 0:
            x2 = ffn(x2, norm2_w[i][None], sh2, sc2, g2, ffn_g[j], ffn_u[j], ffn_d[j], seq=seq)
        else:
            x2 = _moe_layer(x2, norm2_w[i][None], sh2, sc2, g2, moe_router[j],
                            moe_g[j], moe_u[j], moe_d[j], seq=seq, tm_e=896)
    fm = fmod.reshape(bsz, 2, 1, d)
    out = final_norm(x2, final_norm_w[None], fm[:, 0], fm[:, 1], seq=seq)
    return out.reshape(bsz, seq, d)
```

```python
import functools
import math

import jax
import jax.numpy as jnp
from jax import lax
from jax.experimental import pallas as pl
from jax.experimental.pallas import tpu as pltpu

F32 = jnp.float32
BF16 = jnp.bfloat16

EPS = 1e-6
D_MODEL = 1024
DN_HEADS = 4
HEAD_DIM = 128
DN_WIDTH = DN_HEADS * HEAD_DIM
CONV_WIDTH = 4
GM_GROUPS = 4
GM_WIDTH = 512
GM_CHUNK = 128
DN_CHUNK = 128
N_EXPERTS = 8
LANES = 128
C_QKV, C_Z, C_GU, C_GV, C_BA, C_END = 0, 1536, 2048, 2560, 3072, 3200

VMEM_LIMIT = 56 * 1024 * 1024


def _cparams(sem):
    return pltpu.CompilerParams(dimension_semantics=sem, vmem_limit_bytes=VMEM_LIMIT)


def _sigmoid(x):
    return 0.5 + 0.5 * jnp.tanh(0.5 * x)


def _silu(x):
    return x * _sigmoid(x)


def _gelu_tanh(x):
    return 0.5 * x * (1.0 + jnp.tanh(math.sqrt(2.0 / math.pi) * (x + 0.044715 * (x * x * x))))


def _softplus(x):
    return jnp.maximum(x, 0.0) + jnp.log(1.0 + jnp.exp(-jnp.abs(x)))


def _split_hi_lo(x):
    hi = x.astype(BF16)
    lo = (x - hi.astype(F32)).astype(BF16)
    return hi, lo


def _dot(a, b):
    return jnp.dot(a, b, preferred_element_type=F32)


def _ada_kernel(c_ref, w_ref, b_ref, o_ref):
    c = c_ref[...]
    ca = _silu(c)
    c_hi, c_lo = _split_hi_lo(ca)
    w = w_ref[0]
    w_hi, w_lo = _split_hi_lo(w)
    o_ref[0] = _dot(c_hi, w_hi) + _dot(c_lo, w_hi) + _dot(c_hi, w_lo) + b_ref[0]


def ada_mod(c, w, b, tn=2048):
    nl, d, n = w.shape
    bsz = c.shape[0]
    return pl.pallas_call(
        _ada_kernel,
        grid=(nl, n // tn),
        in_specs=[pl.BlockSpec((bsz, d), lambda l, j: (0, 0)),
                  pl.BlockSpec((1, d, tn), lambda l, j: (l, 0, j)),
                  pl.BlockSpec((1, 1, tn), lambda l, j: (l, 0, j))],
        out_specs=pl.BlockSpec((1, bsz, tn), lambda l, j: (l, 0, j)),
        out_shape=jax.ShapeDtypeStruct((nl, bsz, n), F32),
        compiler_params=_cparams(("arbitrary", "arbitrary")),
        name="ada_mod",
    )(c, w, b)


def _norm_mod(x, nw, sh, sc):
    ms = jnp.mean(x * x, axis=-1, keepdims=True)
    return (x * lax.rsqrt(ms + EPS) * nw) * (1.0 + sc) + sh


def _inproj_kernel(x_ref, nw_ref, sh_ref, sc_ref, w_ref, cw_ref, hp_ref, lnw_ref, lnb_ref,
                   ws_ref, bs_ref,
                   q_ref, kb_ref, vb_ref, kt_ref, zs_ref, gc_ref, gct_ref, ogm_ref,
                   ext_ref, *, tm, tiles_per_seq):
    i = pl.program_id(0)
    x = x_ref[...]
    hb = _norm_mod(x, nw_ref[...], sh_ref[0], sc_ref[0]).astype(BF16)

    @pl.when((i % tiles_per_seq) == 0)
    def _():
        ext_ref[0:8, :] = jnp.zeros((8, 3 * DN_WIDTH), F32)

    @pl.when((i % tiles_per_seq) != 0)
    def _():
        ext_ref[0:8, :] = ext_ref[tm:tm + 8, :]

    ext_ref[8:tm + 8, :] = _dot(hb, w_ref[:, C_QKV:C_Z])
    cw = cw_ref[...]
    conv = ext_ref[8:tm + 8, :] * cw[CONV_WIDTH - 1:CONV_WIDTH, :]
    for s in range(1, CONV_WIDTH):
        conv = conv + ext_ref[8 - s:tm + 8 - s, :] * cw[CONV_WIDTH - 1 - s:CONV_WIDTH - s, :]
    act = _silu(conv)

    ba = _dot(hb, w_ref[:, C_BA:C_END])
    beta = _sigmoid(ba)
    hp = hp_ref[...]
    g = hp[0:1, :] * _softplus(ba + hp[1:2, :])
    rowc = lax.broadcasted_iota(jnp.int32, (tm, LANES), 0) % DN_CHUNK
    gc = g
    s = 1
    while s < DN_CHUNK:
        gc = gc + jnp.where(rowc >= s, pltpu.roll(gc, s, 0), 0.0)
        s *= 2
    gc_ref[...] = gc
    gct = gc.T
    for j in range(tm // DN_CHUNK):
        gct_ref[j] = gct[0:8, j * DN_CHUNK:(j + 1) * DN_CHUNK]

    kn_all = []
    for h in range(DN_HEADS):
        lo, hi = h * HEAD_DIM, (h + 1) * HEAD_DIM
        qh = act[:, lo:hi]
        kh = act[:, DN_WIDTH + lo:DN_WIDTH + hi]
        vh = act[:, 2 * DN_WIDTH + lo:2 * DN_WIDTH + hi]
        qn = qh * lax.rsqrt(jnp.sum(qh * qh, axis=-1, keepdims=True) + EPS) * (HEAD_DIM ** -0.5)
        kn = kh * lax.rsqrt(jnp.sum(kh * kh, axis=-1, keepdims=True) + EPS)
        bh = beta[:, h:h + 1]
        q_ref[:, lo:hi] = qn.astype(BF16)
        kb_ref[:, lo:hi] = (kn * bh).astype(BF16)
        vb_ref[:, lo:hi] = (vh * bh).astype(BF16)
        kn_all.append(kn)
    kt = jnp.concatenate(kn_all, axis=1).T
    for j in range(tm // DN_CHUNK):
        kt_ref[j] = kt[:, j * DN_CHUNK:(j + 1) * DN_CHUNK].astype(BF16)

    zs_ref[...] = _silu(_dot(hb, w_ref[:, C_Z:C_GU])).astype(BF16)

    gu = _gelu_tanh(_dot(hb, w_ref[:, C_GU:C_GV]))
    gv = _gelu_tanh(_dot(hb, w_ref[:, C_GV:C_BA]))
    mu = jnp.mean(gv, axis=-1, keepdims=True)
    gvc = gv - mu
    var = jnp.mean(gvc * gvc, axis=-1, keepdims=True)
    gvn = (gvc * lax.rsqrt(var + EPS) * lnw_ref[...] + lnb_ref[...]).astype(BF16)
    for j in range(tm // GM_CHUNK):
        r0, r1 = j * GM_CHUNK, (j + 1) * GM_CHUNK
        for gi in range(GM_GROUPS):
            c0, c1 = gi * LANES, (gi + 1) * LANES
            sp = _dot(ws_ref[gi], gvn[r0:r1, c0:c1]) + bs_ref[gi]
            ogm_ref[r0:r1, c0:c1] = (gu[r0:r1, c0:c1] * sp).astype(BF16)


def inproj(x2, nw, sh, sc, w, cw, hp, lnw, lnb, ws, bs, *, seq, tm=512):
    t, d = x2.shape
    nt = t // tm
    tiles_per_seq = seq // tm
    nc = tm // DN_CHUNK
    row = lambda i: (i, 0)
    const2 = lambda i: (0, 0)
    const3 = lambda i: (0, 0, 0)
    bidx = lambda i: (i // tiles_per_seq, 0, 0)
    out_shape = [
        jax.ShapeDtypeStruct((t, DN_WIDTH), BF16),
        jax.ShapeDtypeStruct((t, DN_WIDTH), BF16),
        jax.ShapeDtypeStruct((t, DN_WIDTH), BF16),
        jax.ShapeDtypeStruct((t // DN_CHUNK, DN_WIDTH, DN_CHUNK), BF16),
        jax.ShapeDtypeStruct((t, DN_WIDTH), BF16),
        jax.ShapeDtypeStruct((t, LANES), F32),
        jax.ShapeDtypeStruct((t // DN_CHUNK, 8, DN_CHUNK), F32),
        jax.ShapeDtypeStruct((t, GM_WIDTH), BF16),
    ]
    out_specs = [
        pl.BlockSpec((tm, DN_WIDTH), row),
        pl.BlockSpec((tm, DN_WIDTH), row),
        pl.BlockSpec((tm, DN_WIDTH), row),
        pl.BlockSpec((nc, DN_WIDTH, DN_CHUNK), lambda i: (i, 0, 0)),
        pl.BlockSpec((tm, DN_WIDTH), row),
        pl.BlockSpec((tm, LANES), row),
        pl.BlockSpec((nc, 8, DN_CHUNK), lambda i: (i, 0, 0)),
        pl.BlockSpec((tm, GM_WIDTH), row),
    ]
    in_specs = [
        pl.BlockSpec((tm, d), row),
        pl.BlockSpec((1, d), const2),
        pl.BlockSpec((1, 1, d), bidx),
        pl.BlockSpec((1, 1, d), bidx),
        pl.BlockSpec((d, C_END), const2),
        pl.BlockSpec((CONV_WIDTH, 3 * DN_WIDTH), const2),
        pl.BlockSpec((8, LANES), const2),
        pl.BlockSpec((1, GM_WIDTH), const2),
        pl.BlockSpec((1, GM_WIDTH), const2),
        pl.BlockSpec((GM_GROUPS, GM_CHUNK, GM_CHUNK), const3),
        pl.BlockSpec((GM_GROUPS, GM_CHUNK, LANES), const3),
    ]
    return pl.pallas_call(
        functools.partial(_inproj_kernel, tm=tm, tiles_per_seq=tiles_per_seq),
        grid=(nt,),
        in_specs=in_specs,
        out_specs=out_specs,
        out_shape=out_shape,
        scratch_shapes=[pltpu.VMEM((tm + 8, 3 * DN_WIDTH), F32)],
        compiler_params=_cparams(("arbitrary",)),
        name="inproj",
    )(x2, nw, sh, sc, w, cw, hp, lnw, lnb, ws, bs)


INV_BASE = 8


def _unit_lower_inverse(a_list, ri, ci):
    c = a_list[0].shape[0]
    same = (ri // INV_BASE) == (ci // INV_BASE)
    eye = jnp.where(ri == ci, 1.0, 0.0)
    ad = [jnp.where(same, a, 0.0) for a in a_list]
    x = [eye - a for a in ad]
    pb = [a.astype(BF16) for a in ad]
    p = [_dot(b, b) for b in pb]
    n = 2
    while True:
        pb = [m.astype(BF16) for m in p]
        x = [xi + _dot(xi.astype(BF16), b) for xi, b in zip(x, pb)]
        n *= 2
        if n >= INV_BASE:
            break
        p = [_dot(b, b) for b in pb]
    s = INV_BASE
    while s < c:
        off = jnp.logical_and((ri // (2 * s)) == (ci // (2 * s)), (ri // s) != (ci // s))
        xb = [xi.astype(BF16) for xi in x]
        y = [_dot(b, jnp.where(off, a, 0.0).astype(BF16)).astype(BF16) for b, a in zip(xb, a_list)]
        x = [xi - _dot(yi, b) for xi, yi, b in zip(x, y, xb)]
        s *= 2
    return x


def _deltanet_kernel(q_ref, kb_ref, vb_ref, kt_ref, zs_ref, gc_ref, gct_ref, nw_ref,
                     o_ref, s_ref, u_ref, w_ref, attn_ref, *, tl, unroll):
    l = pl.program_id(1)

    @pl.when(l == 0)
    def _():
        s_ref[...] = jnp.zeros_like(s_ref)

    c = DN_CHUNK
    ri = lax.broadcasted_iota(jnp.int32, (c, c), 0)
    ci = lax.broadcasted_iota(jnp.int32, (c, c), 1)
    tri = ri >= ci
    strict = ri > ci
    nw = nw_ref[...]
    heads = [(h, h * HEAD_DIM, (h + 1) * HEAD_DIM) for h in range(DN_HEADS)]

    def prep(it, carry):
        items = []
        for sub in range(unroll):
            ch = it * unroll + sub
            r0 = pl.multiple_of(ch * c, c)
            gc = gc_ref[pl.ds(r0, c), :]
            gct = gct_ref[ch]
            for h, lo, hi in heads:
                items.append((ch, r0, h, lo, hi,
                              jnp.broadcast_to(gc[:, 4 + h:5 + h], (c, c)),
                              jnp.broadcast_to(gct[4 + h:5 + h, :], (c, c))))
        kts = [kt_ref[ch, lo:hi, :] for ch, r0, h, lo, hi, _, _ in items]
        kbs = [kb_ref[pl.ds(r0, c), lo:hi] for ch, r0, h, lo, hi, _, _ in items]
        decay = [jnp.where(tri, jnp.exp(jnp.minimum(gcol - grow, 0.0)), 0.0)
                 for *_, gcol, grow in items]
        a = [jnp.where(strict, _dot(kb, kt) * dc, 0.0) for kb, kt, dc in zip(kbs, kts, decay)]
        for (ch, r0, h, lo, hi, _, _), kt, dc in zip(items, kts, decay):
            attn_ref[ch, h] = (_dot(q_ref[pl.ds(r0, c), lo:hi], kt) * dc).astype(BF16)
        tb = [t.astype(BF16) for t in _unit_lower_inverse(a, ri, ci)]
        for (ch, r0, h, lo, hi, gcol, _), t, kb in zip(items, tb, kbs):
            u_ref[pl.ds(r0, c), lo:hi] = _dot(t, vb_ref[pl.ds(r0, c), lo:hi])
            w_ref[pl.ds(r0, c), lo:hi] = _dot(
                t, (kb.astype(F32) * jnp.exp(gcol)).astype(BF16)).astype(BF16)
        return carry

    lax.fori_loop(0, tl // (c * unroll), prep, 0)

    def step(ch, carry):
        r0 = pl.multiple_of(ch * c, c)
        gc = gc_ref[pl.ds(r0, c), :]
        gcol = [jnp.broadcast_to(gc[:, 4 + h:5 + h], (c, c)) for h, lo, hi in heads]
        glast = [jnp.broadcast_to(g[c - 1:c, :], (c, c)) for g in gcol]
        q = [q_ref[pl.ds(r0, c), lo:hi] for h, lo, hi in heads]
        sb = [s_ref[h].astype(BF16) for h, lo, hi in heads]
        ws = [_dot(w_ref[pl.ds(r0, c), lo:hi], s) for (h, lo, hi), s in zip(heads, sb)]
        qs = [_dot(qh, s) for qh, s in zip(q, sb)]
        v_new = [u_ref[pl.ds(r0, c), lo:hi] - w for (h, lo, hi), w in zip(heads, ws)]
        av = [_dot(attn_ref[ch, h], v.astype(BF16)) for (h, lo, hi), v in zip(heads, v_new)]
        kv = [_dot(kt_ref[ch, lo:hi, :], (v * jnp.exp(gl - g)).astype(BF16))
              for (h, lo, hi), v, gl, g in zip(heads, v_new, glast, gcol)]
        for (h, lo, hi), upd, gl in zip(heads, kv, glast):
            s_ref[h] = s_ref[h] * jnp.exp(gl) + upd
        for (h, lo, hi), qsh, avh, g in zip(heads, qs, av, gcol):
            o = qsh * jnp.exp(g) + avh
            ms = jnp.mean(o * o, axis=-1, keepdims=True)
            on = o * lax.rsqrt(ms + EPS) * nw
            zs = zs_ref[pl.ds(r0, c), lo:hi].astype(F32)
            o_ref[pl.ds(r0, c), lo:hi] = (on * zs).astype(BF16)
        return carry

    lax.fori_loop(0, tl // c, step, 0)


def deltanet(q, kb, vb, kt, zs, gc, gct, nw, *, batch, seq, tl=1024):
    t = q.shape[0]
    nl = seq // tl
    nc = tl // DN_CHUNK
    row = lambda b, l: (b * nl + l, 0)
    row3 = lambda b, l: (b * nl + l, 0, 0)
    return pl.pallas_call(
        functools.partial(_deltanet_kernel, tl=tl, unroll=2),
        grid=(batch, nl),
        in_specs=[pl.BlockSpec((tl, DN_WIDTH), row),
                  pl.BlockSpec((tl, DN_WIDTH), row),
                  pl.BlockSpec((tl, DN_WIDTH), row),
                  pl.BlockSpec((nc, DN_WIDTH, DN_CHUNK), row3),
                  pl.BlockSpec((tl, DN_WIDTH), row),
                  pl.BlockSpec((tl, LANES), row),
                  pl.BlockSpec((nc, 8, DN_CHUNK), row3),
                  pl.BlockSpec((1, HEAD_DIM), lambda b, l: (0, 0))],
        out_specs=pl.BlockSpec((tl, DN_WIDTH), row),
        out_shape=jax.ShapeDtypeStruct((t, DN_WIDTH), BF16),
        scratch_shapes=[pltpu.VMEM((DN_HEADS, HEAD_DIM, HEAD_DIM), F32),
                        pltpu.VMEM((tl, DN_WIDTH), F32),
                        pltpu.VMEM((tl, DN_WIDTH), BF16),
                        pltpu.VMEM((nc, DN_HEADS, DN_CHUNK, DN_CHUNK), BF16)],
        compiler_params=_cparams(("arbitrary", "arbitrary")),
        name="deltanet",
    )(q, kb, vb, kt, zs, gc, gct, nw)


def _outproj_kernel(odn_ref, ogm_ref, w_ref, x_ref, g_ref, o_ref):
    mix = _dot(odn_ref[...], w_ref[0:DN_WIDTH, :]) + _dot(ogm_ref[...], w_ref[DN_WIDTH:, :])
    o_ref[...] = x_ref[...] + g_ref[0] * mix


def outproj(odn, ogm, w, x2, g1, *, seq, tm=1024):
    t, d = x2.shape
    tiles_per_seq = seq // tm
    row = lambda i: (i, 0)
    return pl.pallas_call(
        _outproj_kernel,
        grid=(t // tm,),
        in_specs=[pl.BlockSpec((tm, DN_WIDTH), row),
                  pl.BlockSpec((tm, GM_WIDTH), row),
                  pl.BlockSpec((d, d), lambda i: (0, 0)),
                  pl.BlockSpec((tm, d), row),
                  pl.BlockSpec((1, 1, d), lambda i: (i // tiles_per_seq, 0, 0))],
        out_specs=pl.BlockSpec((tm, d), row),
        out_shape=jax.ShapeDtypeStruct((t, d), F32),
        compiler_params=_cparams(("arbitrary",)),
        name="outproj",
    )(odn, ogm, w, x2, g1)


def _ffn_kernel(x_ref, nw_ref, sh_ref, sc_ref, g_ref, wgu_ref, wd_ref, o_ref,
                hb_ref, acc_ref, *, tf):
    f = pl.program_id(1)

    @pl.when(f == 0)
    def _():
        hb_ref[...] = _norm_mod(x_ref[...], nw_ref[...], sh_ref[0], sc_ref[0]).astype(BF16)
        acc_ref[...] = jnp.zeros_like(acc_ref)

    gu = _dot(hb_ref[...], wgu_ref[0])
    acc_ref[...] += _dot((_silu(gu[:, :tf]) * gu[:, tf:]).astype(BF16), wd_ref[...])

    @pl.when(f == pl.num_programs(1) - 1)
    def _():
        o_ref[...] = x_ref[...] + g_ref[0] * acc_ref[...]


FFN_TF = 1408


def pack_gate_up(wg, wu, tf):
    d, dff = wg.shape
    nf = dff // tf
    gu = jnp.concatenate([wg.reshape(d, nf, tf), wu.reshape(d, nf, tf)], axis=-1)
    return jnp.swapaxes(gu, 0, 1)


def ffn(x2, nw, sh, sc, g2, wgu, wd, *, seq, tm=512):
    t, d = x2.shape
    nf, _, tf2 = wgu.shape
    tf = tf2 // 2
    tiles_per_seq = seq // tm
    row = lambda i, f: (i, 0)
    bidx = lambda i, f: (i // tiles_per_seq, 0, 0)
    return pl.pallas_call(
        functools.partial(_ffn_kernel, tf=tf),
        grid=(t // tm, nf),
        in_specs=[pl.BlockSpec((tm, d), row),
                  pl.BlockSpec((1, d), lambda i, f: (0, 0)),
                  pl.BlockSpec((1, 1, d), bidx),
                  pl.BlockSpec((1, 1, d), bidx),
                  pl.BlockSpec((1, 1, d), bidx),
                  pl.BlockSpec((1, d, tf2), lambda i, f: (f, 0, 0)),
                  pl.BlockSpec((tf, d), lambda i, f: (f, 0))],
        out_specs=pl.BlockSpec((tm, d), row),
        out_shape=jax.ShapeDtypeStruct((t, d), F32),
        scratch_shapes=[pltpu.VMEM((tm, d), BF16), pltpu.VMEM((tm, d), F32)],
        compiler_params=_cparams(("arbitrary", "arbitrary")),
        name="ffn",
    )(x2, nw, sh, sc, g2, wgu, wd)


def _router_kernel(x_ref, nw_ref, sh_ref, sc_ref, wr_ref, h_ref, meta_ref, cnt_ref,
                   carry_ref, *, tm):
    i = pl.program_id(0)

    @pl.when(i == 0)
    def _():
        carry_ref[...] = jnp.zeros_like(carry_ref)

    h = _norm_mod(x_ref[...], nw_ref[...], sh_ref[0], sc_ref[0])
    h_ref[...] = h
    h_hi, h_lo = _split_hi_lo(h)
    w_hi, w_lo = _split_hi_lo(wr_ref[...])
    logits = _dot(h_hi, w_hi) + _dot(h_lo, w_hi) + _dot(h_hi, w_lo)
    lane = lax.broadcasted_iota(jnp.int32, (tm, LANES), 1).astype(F32)
    neg = jnp.float32(-jnp.inf)
    lg = jnp.where(lane < N_EXPERTS, logits, neg)
    m1 = jnp.max(lg, axis=-1, keepdims=True)
    i1 = jnp.min(jnp.where(lg == m1, lane, float(LANES)), axis=-1, keepdims=True)
    lg2 = jnp.where(lane == i1, neg, lg)
    m2 = jnp.max(lg2, axis=-1, keepdims=True)
    i2 = jnp.min(jnp.where(lg2 == m2, lane, float(LANES)), axis=-1, keepdims=True)
    e2 = jnp.exp(m2 - m1)
    w1 = 1.0 / (1.0 + e2)
    w2 = e2 / (1.0 + e2)
    oh1 = lane == i1
    oh2 = lane == i2
    onehot = jnp.where(oh1 | oh2, 1.0, 0.0)
    ri = lax.broadcasted_iota(jnp.int32, (tm, tm), 0)
    ci = lax.broadcasted_iota(jnp.int32, (tm, tm), 1)
    lower = jnp.where(ri > ci, 1.0, 0.0).astype(BF16)
    before = _dot(lower, onehot.astype(BF16)) + carry_ref[0:1, :]
    r1 = jnp.sum(jnp.where(oh1, before, 0.0), axis=-1, keepdims=True)
    r2 = jnp.sum(jnp.where(oh2, before, 0.0), axis=-1, keepdims=True)
    total = carry_ref[0:1, :] + jnp.sum(onehot, axis=0, keepdims=True)
    carry_ref[...] = jnp.broadcast_to(total, carry_ref.shape)
    cnt_ref[...] = jnp.broadcast_to(total, cnt_ref.shape)
    meta = jnp.where(lane == 0, i1, 0.0)
    meta = jnp.where(lane == 1, i2, meta)
    meta = jnp.where(lane == 2, w1, meta)
    meta = jnp.where(lane == 3, w2, meta)
    meta = jnp.where(lane == 4, r1, meta)
    meta = jnp.where(lane == 5, r2, meta)
    meta_ref[...] = meta


def router(x2, nw, sh, sc, wr, *, seq, tm=512):
    t, d = x2.shape
    tiles_per_seq = seq // tm
    row = lambda i: (i, 0)
    bidx = lambda i: (i // tiles_per_seq, 0, 0)
    return pl.pallas_call(
        functools.partial(_router_kernel, tm=tm),
        grid=(t // tm,),
        in_specs=[pl.BlockSpec((tm, d), row),
                  pl.BlockSpec((1, d), lambda i: (0, 0)),
                  pl.BlockSpec((1, 1, d), bidx),
                  pl.BlockSpec((1, 1, d), bidx),
                  pl.BlockSpec((d, LANES), lambda i: (0, 0))],
        out_specs=[pl.BlockSpec((tm, d), row),
                   pl.BlockSpec((tm, LANES), row),
                   pl.BlockSpec((8, LANES), lambda i: (0, 0))],
        out_shape=[jax.ShapeDtypeStruct((t, d), F32),
                   jax.ShapeDtypeStruct((t, LANES), F32),
                   jax.ShapeDtypeStruct((8, LANES), F32)],
        scratch_shapes=[pltpu.VMEM((8, LANES), F32)],
        compiler_params=_cparams(("arbitrary",)),
        name="router",
    )(x2, nw, sh, sc, wr)


def _experts_kernel(te_ref, tv_ref, src_cur, src_next, dst_prev, dst_cur, h_hbm,
                    wg_ref, wu_ref, wd_ref, ys_hbm,
                    xbuf, xb_ref, acc_ref, obuf, gate_ref, act_ref, gsem, ssem, *, tm, rows):
    i = pl.program_id(0)
    f = pl.program_id(1)
    last_i = pl.num_programs(0) - 1
    last_f = pl.num_programs(1) - 1
    valid = tv_ref[i] == 1

    def gather_row(idx_ref, r):
        t = idx_ref[0, 0, r]
        pltpu.make_async_copy(h_hbm.at[pl.ds(t, 1), :], xbuf.at[pl.ds(r, 1), :], gsem).start()

    def scatter_row(idx_ref, r):
        d = idx_ref[0, 0, r]
        pltpu.make_async_copy(obuf.at[pl.ds(r, 1), :], ys_hbm.at[pl.ds(d, 1), :], ssem).start()

    def wait_gather():
        pltpu.make_async_copy(h_hbm.at[pl.ds(0, tm), :], xbuf, gsem).wait()

    def wait_scatter():
        pltpu.make_async_copy(obuf, ys_hbm.at[pl.ds(0, tm), :], ssem).wait()

    @pl.when(f == 0)
    def _():
        @pl.when(i == 0)
        def _():
            obuf[...] = jnp.zeros_like(obuf)

            def first(r, carry):
                gather_row(src_cur, r)
                return carry

            lax.fori_loop(0, tm, first, 0)

        wait_gather()
        xb_ref[...] = xbuf[...].astype(BF16)
        acc_ref[...] = jnp.zeros_like(acc_ref)

    def move_rows(lo, hi):
        for j in range(lo, hi):
            gather_row(src_next, f * rows + j)
            scatter_row(dst_prev, f * rows + j)

    cut1, cut2 = rows // 4, rows // 2

    def once(block):
        def body(_, carry):
            block()
            return carry

        lax.fori_loop(0, tv_ref[i], body, 0)

    def block_gate():
        move_rows(0, cut1)
        gate_ref[...] = _dot(xb_ref[...], wg_ref[0])

    def block_up():
        move_rows(cut1, cut2)
        up = _dot(xb_ref[...], wu_ref[0])
        act_ref[...] = (_silu(gate_ref[...]) * up).astype(BF16)

    def block_down():
        move_rows(cut2, rows)
        acc_ref[...] += _dot(act_ref[...], wd_ref[0])

    once(block_gate)
    once(block_up)
    once(block_down)

    @pl.when(jnp.logical_not(valid))
    def _():
        move_rows(0, rows)

    @pl.when(f == last_f)
    def _():
        wait_scatter()
        obuf[...] = acc_ref[...]

        @pl.when(i == last_i)
        def _():
            def final(r, carry):
                scatter_row(dst_cur, r)
                return carry

            lax.fori_loop(0, tm, final, 0)
            wait_scatter()
            wait_gather()


def experts(tile_e, tile_v, src3, dst3, h, wg, wu, wd, *, tm, tf=512):
    nt = src3.shape[0]
    d = h.shape[1]
    dff = wg.shape[2]
    nf = dff // tf
    rows = tm // nf
    assert rows * nf == tm

    def fsel(i, f, tv):
        return jnp.where(tv[i] == 1, f, nf - 1)

    smem = functools.partial(pl.BlockSpec, (1, 1, tm), memory_space=pltpu.SMEM)
    grid_spec = pltpu.PrefetchScalarGridSpec(
        num_scalar_prefetch=2,
        grid=(nt, nf),
        in_specs=[smem(lambda i, f, te, tv: (i, 0, 0)),
                  smem(lambda i, f, te, tv: (jnp.minimum(i + 1, nt - 1), 0, 0)),
                  smem(lambda i, f, te, tv: (jnp.maximum(i - 1, 0), 0, 0)),
                  smem(lambda i, f, te, tv: (i, 0, 0)),
                  pl.BlockSpec(memory_space=pl.ANY),
                  pl.BlockSpec((1, d, tf), lambda i, f, te, tv: (te[i], 0, fsel(i, f, tv))),
                  pl.BlockSpec((1, d, tf), lambda i, f, te, tv: (te[i], 0, fsel(i, f, tv))),
                  pl.BlockSpec((1, tf, d), lambda i, f, te, tv: (te[i], fsel(i, f, tv), 0))],
        out_specs=pl.BlockSpec(memory_space=pl.ANY),
        scratch_shapes=[pltpu.VMEM((tm, d), F32), pltpu.VMEM((tm, d), BF16),
                        pltpu.VMEM((tm, d), F32), pltpu.VMEM((tm, d), F32),
                        pltpu.VMEM((tm, tf), F32), pltpu.VMEM((tm, tf), BF16),
                        pltpu.SemaphoreType.DMA(()), pltpu.SemaphoreType.DMA(())],
    )
    return pl.pallas_call(
        functools.partial(_experts_kernel, tm=tm, rows=rows),
        grid_spec=grid_spec,
        out_shape=jax.ShapeDtypeStruct((nt * tm, d), F32),
        compiler_params=_cparams(("arbitrary", "arbitrary")),
        name="experts",
    )(tile_e, tile_v, src3, src3, dst3, dst3, h, wg, wu, wd)


def _combine_kernel(x_ref, meta_ref, g_ref, y0_ref, y1_ref, o_ref):
    meta = meta_ref[...]
    y = meta[:, 2:3] * y0_ref[...] + meta[:, 3:4] * y1_ref[...]
    o_ref[...] = x_ref[...] + g_ref[0] * y


def combine(x2, meta, g2, ys, *, seq, tm=512):
    t, d = x2.shape
    tiles_per_seq = seq // tm
    nt = t // tm
    row = lambda i: (i, 0)
    return pl.pallas_call(
        _combine_kernel,
        grid=(nt,),
        in_specs=[pl.BlockSpec((tm, d), row),
                  pl.BlockSpec((tm, LANES), row),
                  pl.BlockSpec((1, 1, d), lambda i: (i // tiles_per_seq, 0, 0)),
                  pl.BlockSpec((tm, d), row),
                  pl.BlockSpec((tm, d), lambda i: (nt + i, 0))],
        out_specs=pl.BlockSpec((tm, d), row),
        out_shape=jax.ShapeDtypeStruct((t, d), F32),
        compiler_params=_cparams(("arbitrary",)),
        name="combine",
    )(x2, meta, g2, ys, ys)


def _final_kernel(x_ref, nw_ref, sh_ref, sc_ref, o_ref):
    o_ref[...] = _norm_mod(x_ref[...], nw_ref[...], sh_ref[0], sc_ref[0])


def final_norm(x2, nw, sh, sc, *, seq, tm=1024):
    t, d = x2.shape
    tiles_per_seq = seq // tm
    row = lambda i: (i, 0)
    bidx = lambda i: (i // tiles_per_seq, 0, 0)
    return pl.pallas_call(
        _final_kernel,
        grid=(t // tm,),
        in_specs=[pl.BlockSpec((tm, d), row),
                  pl.BlockSpec((1, d), lambda i: (0, 0)),
                  pl.BlockSpec((1, 1, d), bidx),
                  pl.BlockSpec((1, 1, d), bidx)],
        out_specs=pl.BlockSpec((tm, d), row),
        out_shape=jax.ShapeDtypeStruct((t, d), F32),
        compiler_params=_cparams(("arbitrary",)),
        name="final_norm",
    )(x2, nw, sh, sc)


def _moe_layer(x2, nw, sh, sc, g2, wr, wg, wu, wd, *, seq, tm_e):
    t, d = x2.shape
    wr_pad = jnp.zeros((d, LANES), F32).at[:, :N_EXPERTS].set(wr)
    h, meta, cnt = router(x2, nw, sh, sc, wr_pad, seq=seq)
    counts = cnt[0, :N_EXPERTS].astype(jnp.int32)
    tiles = (counts + tm_e - 1) // tm_e
    tile_end = jnp.cumsum(tiles)
    offs = (tile_end - tiles) * tm_e
    idx = meta[:, 0:2].astype(jnp.int32)
    rank = meta[:, 4:6].astype(jnp.int32)
    dest = (offs[idx] + rank).reshape(-1)
    n_tiles = -(-(2 * t) // tm_e) + N_EXPERTS
    p = n_tiles * tm_e
    tid = jnp.arange(n_tiles, dtype=jnp.int32)
    tile_e = jnp.minimum(jnp.sum(tid[:, None] >= tile_end[None, :], axis=1), N_EXPERTS - 1)
    tile_v = (tid < tile_end[-1]).astype(jnp.int32)
    tile_e = jnp.where(tile_v == 1, tile_e, tile_e[jnp.maximum(tile_end[-1] - 1, 0)]).astype(jnp.int32)
    entry = jnp.zeros((p,), jnp.int32).at[dest].set(
        jnp.arange(1, 2 * t + 1, dtype=jnp.int32), unique_indices=True, mode="promise_in_bounds") - 1
    is_pad = entry < 0
    pad_rank = jnp.cumsum(is_pad.astype(jnp.int32)) - 1
    src = jnp.where(is_pad, 0, entry // 2)
    dst = jnp.where(is_pad, 2 * t + pad_rank, (entry % 2) * t + entry // 2)
    ys = experts(tile_e, tile_v, src.reshape(n_tiles, 1, tm_e), dst.reshape(n_tiles, 1, tm_e),
                 h, wg, wu, wd, tm=tm_e)
    return combine(x2, meta, g2, ys, seq=seq)


def kernel(x, c, ada_w, ada_b, norm1_w, norm2_w, w_in, conv_w, a_log, dt_bias, dn_norm_w, gm_ln_w, gm_ln_b, gm_spatial_w, gm_spatial_b, w_out, ffn_w_gate, ffn_w_up, ffn_w_down, moe_router, moe_w_gate, moe_w_up, moe_w_down, final_ada_w, final_ada_b, final_norm_w):
    bsz, seq, d = x.shape
    depth = ada_w.shape[0]
    t = bsz * seq
    assert d == D_MODEL and seq % 1024 == 0
    x2 = x.reshape(t, d)

    mods = ada_mod(c, ada_w, ada_b[:, None, :])
    fmod = ada_mod(c, final_ada_w[None], final_ada_b[None, None, :])[0]

    w_in_p = jnp.concatenate(
        [w_in[:, :, 0:2048], w_in[:, :, 2056:3080], w_in[:, :, 2048:2056],
         jnp.zeros((depth, d, LANES - 8), F32)], axis=-1).astype(BF16)
    w_out_b = w_out.astype(BF16)
    causal = jnp.tril(jnp.ones((GM_CHUNK, GM_CHUNK), dtype=bool))
    ws = jnp.where(causal, gm_spatial_w, 0.0).astype(BF16)
    bs = jnp.broadcast_to(gm_spatial_b[..., None], (depth, GM_GROUPS, GM_CHUNK, LANES))
    hp = jnp.zeros((depth, 8, LANES), F32)
    hp = hp.at[:, 0, 4:8].set(-jnp.exp(a_log)).at[:, 1, 4:8].set(dt_bias)
    ffn_gu = [pack_gate_up(ffn_w_gate[j].astype(BF16), ffn_w_up[j].astype(BF16), FFN_TF)
              for j in range(ffn_w_gate.shape[0])]
    ffn_d = ffn_w_down.astype(BF16)
    moe_g, moe_u, moe_d = moe_w_gate.astype(BF16), moe_w_up.astype(BF16), moe_w_down.astype(BF16)

    for i in range(depth):
        mod = mods[i].reshape(bsz, 6, 1, d)
        sh1, sc1, g1, sh2, sc2, g2 = (mod[:, k] for k in range(6))
        q, kb, vb, kt, zs, gc, gct, ogm = inproj(
            x2, norm1_w[i][None], sh1, sc1, w_in_p[i], conv_w[i], hp[i],
            gm_ln_w[i][None], gm_ln_b[i][None], ws[i], bs[i], seq=seq)
        odn = deltanet(q, kb, vb, kt, zs, gc, gct, dn_norm_w[i][None], batch=bsz, seq=seq)
        x2 = outproj(odn, ogm, w_out_b[i], x2, g1, seq=seq)
        j = i // 2
        if i % 2 == 0:
            x2 = ffn(x2, norm2_w[i][None], sh2, sc2, g2, ffn_gu[j], ffn_d[j], seq=seq)
        else:
            x2 = _moe_layer(x2, norm2_w[i][None], sh2, sc2, g2, moe_router[j],
                            moe_g[j], moe_u[j], moe_d[j], seq=seq, tm_e=896)
    fm = fmod.reshape(bsz, 2, 1, d)
    out = final_norm(x2, final_norm_w[None], fm[:, 0], fm[:, 1], seq=seq)
    return out.reshape(bsz, seq, d)
```

```python
import functools
import math

import jax
import jax.numpy as jnp
from jax import lax
from jax.experimental import pallas as pl
from jax.experimental.pallas import tpu as pltpu

F32 = jnp.float32
BF16 = jnp.bfloat16

EPS = 1e-6
D_MODEL = 1024
DN_HEADS = 4
HEAD_DIM = 128
DN_WIDTH = DN_HEADS * HEAD_DIM
CONV_WIDTH = 4
GM_GROUPS = 4
GM_WIDTH = 512
GM_CHUNK = 128
DN_CHUNK = 128
N_EXPERTS = 8
LANES = 128
ROW_TILES = D_MODEL // LANES
C_QKV, C_Z, C_GU, C_GV, C_BA, C_END = 0, 1536, 2048, 2560, 3072, 3200

VMEM_LIMIT = 56 * 1024 * 1024


def _cparams(sem):
    return pltpu.CompilerParams(dimension_semantics=sem, vmem_limit_bytes=VMEM_LIMIT)


def _sigmoid(x):
    return 0.5 + 0.5 * jnp.tanh(0.5 * x)


def _silu(x):
    return x * _sigmoid(x)


def _gelu_tanh(x):
    return 0.5 * x * (1.0 + jnp.tanh(math.sqrt(2.0 / math.pi) * (x + 0.044715 * (x * x * x))))


def _softplus(x):
    return jnp.maximum(x, 0.0) + jnp.log(1.0 + jnp.exp(-jnp.abs(x)))


def _split_hi_lo(x):
    hi = x.astype(BF16)
    lo = (x - hi.astype(F32)).astype(BF16)
    return hi, lo


def _dot(a, b):
    return jnp.dot(a, b, preferred_element_type=F32)


def _ada_kernel(c_ref, w_ref, b_ref, o_ref):
    c = c_ref[...]
    ca = _silu(c)
    c_hi, c_lo = _split_hi_lo(ca)
    w = w_ref[0]
    w_hi, w_lo = _split_hi_lo(w)
    o_ref[0] = _dot(c_hi, w_hi) + _dot(c_lo, w_hi) + _dot(c_hi, w_lo) + b_ref[0]


def ada_mod(c, w, b, tn=2048):
    nl, d, n = w.shape
    bsz = c.shape[0]
    return pl.pallas_call(
        _ada_kernel,
        grid=(nl, n // tn),
        in_specs=[pl.BlockSpec((bsz, d), lambda l, j: (0, 0)),
                  pl.BlockSpec((1, d, tn), lambda l, j: (l, 0, j)),
                  pl.BlockSpec((1, 1, tn), lambda l, j: (l, 0, j))],
        out_specs=pl.BlockSpec((1, bsz, tn), lambda l, j: (l, 0, j)),
        out_shape=jax.ShapeDtypeStruct((nl, bsz, n), F32),
        compiler_params=_cparams(("arbitrary", "arbitrary")),
        name="ada_mod",
    )(c, w, b)


def _token_rows(t):
    return pl.ds(pl.multiple_of(t * ROW_TILES, ROW_TILES), ROW_TILES)


def _norm_mod(x, nw, sh, sc):
    ms = jnp.mean(x * x, axis=-1, keepdims=True)
    return (x * lax.rsqrt(ms + EPS) * nw) * (1.0 + sc) + sh


def _inproj_kernel(x_ref, nw_ref, sh_ref, sc_ref, w_ref, cw_ref, hp_ref, lnw_ref, lnb_ref,
                   ws_ref, bs_ref,
                   q_ref, kb_ref, vb_ref, kt_ref, zs_ref, gc_ref, gct_ref, ogm_ref,
                   ext_ref, *, tm, tiles_per_seq):
    i = pl.program_id(0)
    x = x_ref[...]
    hb = _norm_mod(x, nw_ref[...], sh_ref[0], sc_ref[0]).astype(BF16)

    @pl.when((i % tiles_per_seq) == 0)
    def _():
        ext_ref[0:8, :] = jnp.zeros((8, 3 * DN_WIDTH), F32)

    @pl.when((i % tiles_per_seq) != 0)
    def _():
        ext_ref[0:8, :] = ext_ref[tm:tm + 8, :]

    ext_ref[8:tm + 8, :] = _dot(hb, w_ref[:, C_QKV:C_Z])
    cw = cw_ref[...]
    conv = ext_ref[8:tm + 8, :] * cw[CONV_WIDTH - 1:CONV_WIDTH, :]
    for s in range(1, CONV_WIDTH):
        conv = conv + ext_ref[8 - s:tm + 8 - s, :] * cw[CONV_WIDTH - 1 - s:CONV_WIDTH - s, :]
    act = _silu(conv)

    ba = _dot(hb, w_ref[:, C_BA:C_END])
    beta = _sigmoid(ba)
    hp = hp_ref[...]
    g = hp[0:1, :] * _softplus(ba + hp[1:2, :])
    rowc = lax.broadcasted_iota(jnp.int32, (tm, LANES), 0) % DN_CHUNK
    gc = g
    s = 1
    while s < DN_CHUNK:
        gc = gc + jnp.where(rowc >= s, pltpu.roll(gc, s, 0), 0.0)
        s *= 2
    gc_ref[...] = gc
    gct = gc.T
    for j in range(tm // DN_CHUNK):
        gct_ref[j] = gct[0:8, j * DN_CHUNK:(j + 1) * DN_CHUNK]

    kn_all = []
    for h in range(DN_HEADS):
        lo, hi = h * HEAD_DIM, (h + 1) * HEAD_DIM
        qh = act[:, lo:hi]
        kh = act[:, DN_WIDTH + lo:DN_WIDTH + hi]
        vh = act[:, 2 * DN_WIDTH + lo:2 * DN_WIDTH + hi]
        qn = qh * lax.rsqrt(jnp.sum(qh * qh, axis=-1, keepdims=True) + EPS) * (HEAD_DIM ** -0.5)
        kn = kh * lax.rsqrt(jnp.sum(kh * kh, axis=-1, keepdims=True) + EPS)
        bh = beta[:, h:h + 1]
        q_ref[:, lo:hi] = qn.astype(BF16)
        kb_ref[:, lo:hi] = (kn * bh).astype(BF16)
        vb_ref[:, lo:hi] = (vh * bh).astype(BF16)
        kn_all.append(kn)
    kt = jnp.concatenate(kn_all, axis=1).T
    for j in range(tm // DN_CHUNK):
        kt_ref[j] = kt[:, j * DN_CHUNK:(j + 1) * DN_CHUNK].astype(BF16)

    zs_ref[...] = _silu(_dot(hb, w_ref[:, C_Z:C_GU])).astype(BF16)

    gu = _gelu_tanh(_dot(hb, w_ref[:, C_GU:C_GV]))
    gv = _gelu_tanh(_dot(hb, w_ref[:, C_GV:C_BA]))
    mu = jnp.mean(gv, axis=-1, keepdims=True)
    gvc = gv - mu
    var = jnp.mean(gvc * gvc, axis=-1, keepdims=True)
    gvn = (gvc * lax.rsqrt(var + EPS) * lnw_ref[...] + lnb_ref[...]).astype(BF16)
    for j in range(tm // GM_CHUNK):
        r0, r1 = j * GM_CHUNK, (j + 1) * GM_CHUNK
        for gi in range(GM_GROUPS):
            c0, c1 = gi * LANES, (gi + 1) * LANES
            sp = _dot(ws_ref[gi], gvn[r0:r1, c0:c1]) + bs_ref[gi]
            ogm_ref[r0:r1, c0:c1] = (gu[r0:r1, c0:c1] * sp).astype(BF16)


def inproj(x2, nw, sh, sc, w, cw, hp, lnw, lnb, ws, bs, *, seq, tm=512):
    t, d = x2.shape
    nt = t // tm
    tiles_per_seq = seq // tm
    nc = tm // DN_CHUNK
    row = lambda i: (i, 0)
    const2 = lambda i: (0, 0)
    const3 = lambda i: (0, 0, 0)
    bidx = lambda i: (i // tiles_per_seq, 0, 0)
    out_shape = [
        jax.ShapeDtypeStruct((t, DN_WIDTH), BF16),
        jax.ShapeDtypeStruct((t, DN_WIDTH), BF16),
        jax.ShapeDtypeStruct((t, DN_WIDTH), BF16),
        jax.ShapeDtypeStruct((t // DN_CHUNK, DN_WIDTH, DN_CHUNK), BF16),
        jax.ShapeDtypeStruct((t, DN_WIDTH), BF16),
        jax.ShapeDtypeStruct((t, LANES), F32),
        jax.ShapeDtypeStruct((t // DN_CHUNK, 8, DN_CHUNK), F32),
        jax.ShapeDtypeStruct((t, GM_WIDTH), BF16),
    ]
    out_specs = [
        pl.BlockSpec((tm, DN_WIDTH), row),
        pl.BlockSpec((tm, DN_WIDTH), row),
        pl.BlockSpec((tm, DN_WIDTH), row),
        pl.BlockSpec((nc, DN_WIDTH, DN_CHUNK), lambda i: (i, 0, 0)),
        pl.BlockSpec((tm, DN_WIDTH), row),
        pl.BlockSpec((tm, LANES), row),
        pl.BlockSpec((nc, 8, DN_CHUNK), lambda i: (i, 0, 0)),
        pl.BlockSpec((tm, GM_WIDTH), row),
    ]
    in_specs = [
        pl.BlockSpec((tm, d), row),
        pl.BlockSpec((1, d), const2),
        pl.BlockSpec((1, 1, d), bidx),
        pl.BlockSpec((1, 1, d), bidx),
        pl.BlockSpec((d, C_END), const2),
        pl.BlockSpec((CONV_WIDTH, 3 * DN_WIDTH), const2),
        pl.BlockSpec((8, LANES), const2),
        pl.BlockSpec((1, GM_WIDTH), const2),
        pl.BlockSpec((1, GM_WIDTH), const2),
        pl.BlockSpec((GM_GROUPS, GM_CHUNK, GM_CHUNK), const3),
        pl.BlockSpec((GM_GROUPS, GM_CHUNK, LANES), const3),
    ]
    return pl.pallas_call(
        functools.partial(_inproj_kernel, tm=tm, tiles_per_seq=tiles_per_seq),
        grid=(nt,),
        in_specs=in_specs,
        out_specs=out_specs,
        out_shape=out_shape,
        scratch_shapes=[pltpu.VMEM((tm + 8, 3 * DN_WIDTH), F32)],
        compiler_params=_cparams(("arbitrary",)),
        name="inproj",
    )(x2, nw, sh, sc, w, cw, hp, lnw, lnb, ws, bs)


INV_BASE = 8


def _unit_lower_inverse(a_list, ri, ci):
    c = a_list[0].shape[0]
    same = (ri // INV_BASE) == (ci // INV_BASE)
    eye = jnp.where(ri == ci, 1.0, 0.0)
    ad = [jnp.where(same, a, 0.0) for a in a_list]
    x = [eye - a for a in ad]
    pb = [a.astype(BF16) for a in ad]
    p = [_dot(b, b) for b in pb]
    n = 2
    while True:
        pb = [m.astype(BF16) for m in p]
        x = [xi + _dot(xi.astype(BF16), b) for xi, b in zip(x, pb)]
        n *= 2
        if n >= INV_BASE:
            break
        p = [_dot(b, b) for b in pb]
    s = INV_BASE
    while s < c:
        off = jnp.logical_and((ri // (2 * s)) == (ci // (2 * s)), (ri // s) != (ci // s))
        xb = [xi.astype(BF16) for xi in x]
        y = [_dot(b, jnp.where(off, a, 0.0).astype(BF16)).astype(BF16) for b, a in zip(xb, a_list)]
        x = [xi - _dot(yi, b) for xi, yi, b in zip(x, y, xb)]
        s *= 2
    return x


def _deltanet_kernel(q_ref, kb_ref, vb_ref, kt_ref, zs_ref, gc_ref, gct_ref, nw_ref,
                     o_ref, s_ref, u_ref, w_ref, attn_ref, *, tl, unroll):
    l = pl.program_id(1)

    @pl.when(l == 0)
    def _():
        s_ref[...] = jnp.zeros_like(s_ref)

    c = DN_CHUNK
    ri = lax.broadcasted_iota(jnp.int32, (c, c), 0)
    ci = lax.broadcasted_iota(jnp.int32, (c, c), 1)
    tri = ri >= ci
    strict = ri > ci
    nw = nw_ref[...]
    heads = [(h, h * HEAD_DIM, (h + 1) * HEAD_DIM) for h in range(DN_HEADS)]

    def prep(it, carry):
        items = []
        for sub in range(unroll):
            ch = it * unroll + sub
            r0 = pl.multiple_of(ch * c, c)
            gc = gc_ref[pl.ds(r0, c), :]
            gct = gct_ref[ch]
            for h, lo, hi in heads:
                items.append((ch, r0, h, lo, hi,
                              jnp.broadcast_to(gc[:, 4 + h:5 + h], (c, c)),
                              jnp.broadcast_to(gct[4 + h:5 + h, :], (c, c))))
        kts = [kt_ref[ch, lo:hi, :] for ch, r0, h, lo, hi, _, _ in items]
        kbs = [kb_ref[pl.ds(r0, c), lo:hi] for ch, r0, h, lo, hi, _, _ in items]
        decay = [jnp.where(tri, jnp.exp(jnp.minimum(gcol - grow, 0.0)), 0.0)
                 for *_, gcol, grow in items]
        a = [jnp.where(strict, _dot(kb, kt) * dc, 0.0) for kb, kt, dc in zip(kbs, kts, decay)]
        for (ch, r0, h, lo, hi, _, _), kt, dc in zip(items, kts, decay):
            attn_ref[ch, h] = (_dot(q_ref[pl.ds(r0, c), lo:hi], kt) * dc).astype(BF16)
        tb = [t.astype(BF16) for t in _unit_lower_inverse(a, ri, ci)]
        for (ch, r0, h, lo, hi, gcol, _), t, kb in zip(items, tb, kbs):
            u_ref[pl.ds(r0, c), lo:hi] = _dot(t, vb_ref[pl.ds(r0, c), lo:hi])
            w_ref[pl.ds(r0, c), lo:hi] = _dot(
                t, (kb.astype(F32) * jnp.exp(gcol)).astype(BF16)).astype(BF16)
        return carry

    lax.fori_loop(0, tl // (c * unroll), prep, 0)

    def step(ch, carry):
        r0 = pl.multiple_of(ch * c, c)
        gc = gc_ref[pl.ds(r0, c), :]
        gcol = [jnp.broadcast_to(gc[:, 4 + h:5 + h], (c, c)) for h, lo, hi in heads]
        glast = [jnp.broadcast_to(g[c - 1:c, :], (c, c)) for g in gcol]
        q = [q_ref[pl.ds(r0, c), lo:hi] for h, lo, hi in heads]
        sb = [s_ref[h].astype(BF16) for h, lo, hi in heads]
        ws = [_dot(w_ref[pl.ds(r0, c), lo:hi], s) for (h, lo, hi), s in zip(heads, sb)]
        qs = [_dot(qh, s) for qh, s in zip(q, sb)]
        v_new = [u_ref[pl.ds(r0, c), lo:hi] - w for (h, lo, hi), w in zip(heads, ws)]
        av = [_dot(attn_ref[ch, h], v.astype(BF16)) for (h, lo, hi), v in zip(heads, v_new)]
        kv = [_dot(kt_ref[ch, lo:hi, :], (v * jnp.exp(gl - g)).astype(BF16))
              for (h, lo, hi), v, gl, g in zip(heads, v_new, glast, gcol)]
        for (h, lo, hi), upd, gl in zip(heads, kv, glast):
            s_ref[h] = s_ref[h] * jnp.exp(gl) + upd
        for (h, lo, hi), qsh, avh, g in zip(heads, qs, av, gcol):
            o = qsh * jnp.exp(g) + avh
            ms = jnp.mean(o * o, axis=-1, keepdims=True)
            on = o * lax.rsqrt(ms + EPS) * nw
            zs = zs_ref[pl.ds(r0, c), lo:hi].astype(F32)
            o_ref[pl.ds(r0, c), lo:hi] = (on * zs).astype(BF16)
        return carry

    lax.fori_loop(0, tl // c, step, 0)


def deltanet(q, kb, vb, kt, zs, gc, gct, nw, *, batch, seq, tl=1024):
    t = q.shape[0]
    nl = seq // tl
    nc = tl // DN_CHUNK
    row = lambda b, l: (b * nl + l, 0)
    row3 = lambda b, l: (b * nl + l, 0, 0)
    return pl.pallas_call(
        functools.partial(_deltanet_kernel, tl=tl, unroll=4),
        grid=(batch, nl),
        in_specs=[pl.BlockSpec((tl, DN_WIDTH), row),
                  pl.BlockSpec((tl, DN_WIDTH), row),
                  pl.BlockSpec((tl, DN_WIDTH), row),
                  pl.BlockSpec((nc, DN_WIDTH, DN_CHUNK), row3),
                  pl.BlockSpec((tl, DN_WIDTH), row),
                  pl.BlockSpec((tl, LANES), row),
                  pl.BlockSpec((nc, 8, DN_CHUNK), row3),
                  pl.BlockSpec((1, HEAD_DIM), lambda b, l: (0, 0))],
        out_specs=pl.BlockSpec((tl, DN_WIDTH), row),
        out_shape=jax.ShapeDtypeStruct((t, DN_WIDTH), BF16),
        scratch_shapes=[pltpu.VMEM((DN_HEADS, HEAD_DIM, HEAD_DIM), F32),
                        pltpu.VMEM((tl, DN_WIDTH), F32),
                        pltpu.VMEM((tl, DN_WIDTH), BF16),
                        pltpu.VMEM((nc, DN_HEADS, DN_CHUNK, DN_CHUNK), BF16)],
        compiler_params=_cparams(("arbitrary", "arbitrary")),
        name="deltanet",
    )(q, kb, vb, kt, zs, gc, gct, nw)


def _outproj_kernel(odn_ref, ogm_ref, w_ref, x_ref, g_ref, o_ref):
    mix = _dot(odn_ref[...], w_ref[0:DN_WIDTH, :]) + _dot(ogm_ref[...], w_ref[DN_WIDTH:, :])
    o_ref[...] = x_ref[...] + g_ref[0] * mix


def outproj(odn, ogm, w, x2, g1, *, seq, tm=1024):
    t, d = x2.shape
    tiles_per_seq = seq // tm
    row = lambda i: (i, 0)
    return pl.pallas_call(
        _outproj_kernel,
        grid=(t // tm,),
        in_specs=[pl.BlockSpec((tm, DN_WIDTH), row),
                  pl.BlockSpec((tm, GM_WIDTH), row),
                  pl.BlockSpec((d, d), lambda i: (0, 0)),
                  pl.BlockSpec((tm, d), row),
                  pl.BlockSpec((1, 1, d), lambda i: (i // tiles_per_seq, 0, 0))],
        out_specs=pl.BlockSpec((tm, d), row),
        out_shape=jax.ShapeDtypeStruct((t, d), F32),
        compiler_params=_cparams(("arbitrary",)),
        name="outproj",
    )(odn, ogm, w, x2, g1)


def _ffn_kernel(x_ref, nw_ref, sh_ref, sc_ref, g_ref, wgu_ref, wd_ref, o_ref,
                hb_ref, acc_ref, *, tf):
    f = pl.program_id(1)

    @pl.when(f == 0)
    def _():
        hb_ref[...] = _norm_mod(x_ref[...], nw_ref[...], sh_ref[0], sc_ref[0]).astype(BF16)
        acc_ref[...] = jnp.zeros_like(acc_ref)

    gu = _dot(hb_ref[...], wgu_ref[0])
    acc_ref[...] += _dot((_silu(gu[:, :tf]) * gu[:, tf:]).astype(BF16), wd_ref[...])

    @pl.when(f == pl.num_programs(1) - 1)
    def _():
        o_ref[...] = x_ref[...] + g_ref[0] * acc_ref[...]


FFN_TF = 1408


def pack_gate_up(wg, wu, tf):
    d, dff = wg.shape
    nf = dff // tf
    gu = jnp.concatenate([wg.reshape(d, nf, tf), wu.reshape(d, nf, tf)], axis=-1)
    return jnp.swapaxes(gu, 0, 1)


def ffn(x2, nw, sh, sc, g2, wgu, wd, *, seq, tm=512):
    t, d = x2.shape
    nf, _, tf2 = wgu.shape
    tf = tf2 // 2
    tiles_per_seq = seq // tm
    row = lambda i, f: (i, 0)
    bidx = lambda i, f: (i // tiles_per_seq, 0, 0)
    return pl.pallas_call(
        functools.partial(_ffn_kernel, tf=tf),
        grid=(t // tm, nf),
        in_specs=[pl.BlockSpec((tm, d), row),
                  pl.BlockSpec((1, d), lambda i, f: (0, 0)),
                  pl.BlockSpec((1, 1, d), bidx),
                  pl.BlockSpec((1, 1, d), bidx),
                  pl.BlockSpec((1, 1, d), bidx),
                  pl.BlockSpec((1, d, tf2), lambda i, f: (f, 0, 0)),
                  pl.BlockSpec((tf, d), lambda i, f: (f, 0))],
        out_specs=pl.BlockSpec((tm, d), row),
        out_shape=jax.ShapeDtypeStruct((t, d), F32),
        scratch_shapes=[pltpu.VMEM((tm, d), BF16), pltpu.VMEM((tm, d), F32)],
        compiler_params=_cparams(("arbitrary", "arbitrary")),
        name="ffn",
    )(x2, nw, sh, sc, g2, wgu, wd)


def _router_kernel(x_ref, nw_ref, sh_ref, sc_ref, wr_ref, h_ref, meta_ref, cnt_ref,
                   carry_ref, *, tm):
    i = pl.program_id(0)

    @pl.when(i == 0)
    def _():
        carry_ref[...] = jnp.zeros_like(carry_ref)

    h = _norm_mod(x_ref[...], nw_ref[...], sh_ref[0], sc_ref[0])
    for k in range(ROW_TILES):
        h_ref[pl.ds(k, tm, stride=ROW_TILES), :] = h[:, k * LANES:(k + 1) * LANES]
    h_hi, h_lo = _split_hi_lo(h)
    w_hi, w_lo = _split_hi_lo(wr_ref[...])
    logits = _dot(h_hi, w_hi) + _dot(h_lo, w_hi) + _dot(h_hi, w_lo)
    lane = lax.broadcasted_iota(jnp.int32, (tm, LANES), 1).astype(F32)
    neg = jnp.float32(-jnp.inf)
    lg = jnp.where(lane < N_EXPERTS, logits, neg)
    m1 = jnp.max(lg, axis=-1, keepdims=True)
    i1 = jnp.min(jnp.where(lg == m1, lane, float(LANES)), axis=-1, keepdims=True)
    lg2 = jnp.where(lane == i1, neg, lg)
    m2 = jnp.max(lg2, axis=-1, keepdims=True)
    i2 = jnp.min(jnp.where(lg2 == m2, lane, float(LANES)), axis=-1, keepdims=True)
    e2 = jnp.exp(m2 - m1)
    w1 = 1.0 / (1.0 + e2)
    w2 = e2 / (1.0 + e2)
    oh1 = lane == i1
    oh2 = lane == i2
    onehot = jnp.where(oh1 | oh2, 1.0, 0.0)
    ri = lax.broadcasted_iota(jnp.int32, (tm, tm), 0)
    ci = lax.broadcasted_iota(jnp.int32, (tm, tm), 1)
    lower = jnp.where(ri > ci, 1.0, 0.0).astype(BF16)
    before = _dot(lower, onehot.astype(BF16)) + carry_ref[0:1, :]
    r1 = jnp.sum(jnp.where(oh1, before, 0.0), axis=-1, keepdims=True)
    r2 = jnp.sum(jnp.where(oh2, before, 0.0), axis=-1, keepdims=True)
    total = carry_ref[0:1, :] + jnp.sum(onehot, axis=0, keepdims=True)
    carry_ref[...] = jnp.broadcast_to(total, carry_ref.shape)
    cnt_ref[...] = jnp.broadcast_to(total, cnt_ref.shape)
    meta = jnp.where(lane == 0, i1, 0.0)
    meta = jnp.where(lane == 1, i2, meta)
    meta = jnp.where(lane == 2, w1, meta)
    meta = jnp.where(lane == 3, w2, meta)
    meta = jnp.where(lane == 4, r1, meta)
    meta = jnp.where(lane == 5, r2, meta)
    meta_ref[...] = meta


def router(x2, nw, sh, sc, wr, *, seq, tm=512):
    t, d = x2.shape
    tiles_per_seq = seq // tm
    row = lambda i: (i, 0)
    bidx = lambda i: (i // tiles_per_seq, 0, 0)
    return pl.pallas_call(
        functools.partial(_router_kernel, tm=tm),
        grid=(t // tm,),
        in_specs=[pl.BlockSpec((tm, d), row),
                  pl.BlockSpec((1, d), lambda i: (0, 0)),
                  pl.BlockSpec((1, 1, d), bidx),
                  pl.BlockSpec((1, 1, d), bidx),
                  pl.BlockSpec((d, LANES), lambda i: (0, 0))],
        out_specs=[pl.BlockSpec((tm * ROW_TILES, LANES), row),
                   pl.BlockSpec((tm, LANES), row),
                   pl.BlockSpec((8, LANES), lambda i: (0, 0))],
        out_shape=[jax.ShapeDtypeStruct((t * ROW_TILES, LANES), F32),
                   jax.ShapeDtypeStruct((t, LANES), F32),
                   jax.ShapeDtypeStruct((8, LANES), F32)],
        scratch_shapes=[pltpu.VMEM((8, LANES), F32)],
        compiler_params=_cparams(("arbitrary",)),
        name="router",
    )(x2, nw, sh, sc, wr)


def _experts_kernel(te_ref, tv_ref, src_cur, src_next, dst_prev, dst_cur, h_hbm,
                    wg_ref, wu_ref, wd_ref, ys_hbm,
                    xbuf, xb_ref, acc_ref, obuf, gate_ref, act_ref, gsem, ssem, *, tm, rows):
    i = pl.program_id(0)
    f = pl.program_id(1)
    last_i = pl.num_programs(0) - 1
    last_f = pl.num_programs(1) - 1
    valid = tv_ref[i] == 1

    def gather_row(idx_ref, r):
        t = idx_ref[0, 0, r]
        pltpu.make_async_copy(h_hbm.at[_token_rows(t), :], xbuf.at[_token_rows(r), :], gsem).start()

    def scatter_row(idx_ref, r):
        d = idx_ref[0, 0, r]
        pltpu.make_async_copy(obuf.at[_token_rows(r), :], ys_hbm.at[_token_rows(d), :], ssem).start()

    def wait_gather():
        pltpu.make_async_copy(h_hbm.at[pl.ds(0, tm * ROW_TILES), :], xbuf, gsem).wait()

    def wait_scatter():
        pltpu.make_async_copy(obuf, ys_hbm.at[pl.ds(0, tm * ROW_TILES), :], ssem).wait()

    @pl.when(f == 0)
    def _():
        @pl.when(i == 0)
        def _():
            obuf[...] = jnp.zeros_like(obuf)

            def first(r, carry):
                gather_row(src_cur, r)
                return carry

            lax.fori_loop(0, tm, first, 0)

        wait_gather()
        for k in range(ROW_TILES):
            xb_ref[:, k * LANES:(k + 1) * LANES] = xbuf[pl.ds(k, tm, stride=ROW_TILES), :].astype(BF16)
        acc_ref[...] = jnp.zeros_like(acc_ref)

    def move_rows(lo, hi):
        for j in range(lo, hi):
            gather_row(src_next, f * rows + j)
            scatter_row(dst_prev, f * rows + j)

    cut1, cut2 = rows // 4, rows // 2

    def once(block):
        def body(_, carry):
            block()
            return carry

        lax.fori_loop(0, tv_ref[i], body, 0)

    def block_gate():
        move_rows(0, cut1)
        gate_ref[...] = _dot(xb_ref[...], wg_ref[0])

    def block_up():
        move_rows(cut1, cut2)
        up = _dot(xb_ref[...], wu_ref[0])
        act_ref[...] = (_silu(gate_ref[...]) * up).astype(BF16)

    def block_down():
        move_rows(cut2, rows)
        acc_ref[...] += _dot(act_ref[...], wd_ref[0])

    once(block_gate)
    once(block_up)
    once(block_down)

    @pl.when(jnp.logical_not(valid))
    def _():
        move_rows(0, rows)

    @pl.when(f == last_f)
    def _():
        wait_scatter()
        for k in range(ROW_TILES):
            obuf[pl.ds(k, tm, stride=ROW_TILES), :] = acc_ref[:, k * LANES:(k + 1) * LANES]

        @pl.when(i == last_i)
        def _():
            def final(r, carry):
                scatter_row(dst_cur, r)
                return carry

            lax.fori_loop(0, tm, final, 0)
            wait_scatter()
            wait_gather()


def experts(tile_e, tile_v, src3, dst3, h, wg, wu, wd, *, tm, tf=1792):
    nt = src3.shape[0]
    d = D_MODEL
    flat = (tm * ROW_TILES, LANES)
    dff = wg.shape[2]
    nf = dff // tf
    rows = tm // nf
    assert rows * nf == tm

    def fsel(i, f, tv):
        return jnp.where(tv[i] == 1, f, nf - 1)

    smem = functools.partial(pl.BlockSpec, (1, 1, tm), memory_space=pltpu.SMEM)
    grid_spec = pltpu.PrefetchScalarGridSpec(
        num_scalar_prefetch=2,
        grid=(nt, nf),
        in_specs=[smem(lambda i, f, te, tv: (i, 0, 0)),
                  smem(lambda i, f, te, tv: (jnp.minimum(i + 1, nt - 1), 0, 0)),
                  smem(lambda i, f, te, tv: (jnp.maximum(i - 1, 0), 0, 0)),
                  smem(lambda i, f, te, tv: (i, 0, 0)),
                  pl.BlockSpec(memory_space=pl.ANY),
                  pl.BlockSpec((1, d, tf), lambda i, f, te, tv: (te[i], 0, fsel(i, f, tv))),
                  pl.BlockSpec((1, d, tf), lambda i, f, te, tv: (te[i], 0, fsel(i, f, tv))),
                  pl.BlockSpec((1, tf, d), lambda i, f, te, tv: (te[i], fsel(i, f, tv), 0))],
        out_specs=pl.BlockSpec(memory_space=pl.ANY),
        scratch_shapes=[pltpu.VMEM(flat, F32), pltpu.VMEM((tm, d), BF16),
                        pltpu.VMEM((tm, d), F32), pltpu.VMEM(flat, F32),
                        pltpu.VMEM((tm, tf), F32), pltpu.VMEM((tm, tf), BF16),
                        pltpu.SemaphoreType.DMA(()), pltpu.SemaphoreType.DMA(())],
    )
    return pl.pallas_call(
        functools.partial(_experts_kernel, tm=tm, rows=rows),
        grid_spec=grid_spec,
        out_shape=jax.ShapeDtypeStruct((nt * tm * ROW_TILES, LANES), F32),
        compiler_params=_cparams(("arbitrary", "arbitrary")),
        name="experts",
    )(tile_e, tile_v, src3, src3, dst3, dst3, h, wg, wu, wd)


def _combine_kernel(x_ref, meta_ref, g_ref, y0_ref, y1_ref, o_ref, *, tm):
    meta = meta_ref[...]
    w0, w1 = meta[:, 2:3], meta[:, 3:4]
    for k in range(ROW_TILES):
        cols = slice(k * LANES, (k + 1) * LANES)
        rows = pl.ds(k, tm, stride=ROW_TILES)
        y = w0 * y0_ref[rows, :] + w1 * y1_ref[rows, :]
        o_ref[:, cols] = x_ref[:, cols] + g_ref[0, :, cols] * y


def combine(x2, meta, g2, ys, *, seq, tm=512):
    t, d = x2.shape
    tiles_per_seq = seq // tm
    nt = t // tm
    row = lambda i: (i, 0)
    return pl.pallas_call(
        functools.partial(_combine_kernel, tm=tm),
        grid=(nt,),
        in_specs=[pl.BlockSpec((tm, d), row),
                  pl.BlockSpec((tm, LANES), row),
                  pl.BlockSpec((1, 1, d), lambda i: (i // tiles_per_seq, 0, 0)),
                  pl.BlockSpec((tm * ROW_TILES, LANES), row),
                  pl.BlockSpec((tm * ROW_TILES, LANES), lambda i: (nt + i, 0))],
        out_specs=pl.BlockSpec((tm, d), row),
        out_shape=jax.ShapeDtypeStruct((t, d), F32),
        compiler_params=_cparams(("arbitrary",)),
        name="combine",
    )(x2, meta, g2, ys, ys)


def _final_kernel(x_ref, nw_ref, sh_ref, sc_ref, o_ref):
    o_ref[...] = _norm_mod(x_ref[...], nw_ref[...], sh_ref[0], sc_ref[0])


def final_norm(x2, nw, sh, sc, *, seq, tm=1024):
    t, d = x2.shape
    tiles_per_seq = seq // tm
    row = lambda i: (i, 0)
    bidx = lambda i: (i // tiles_per_seq, 0, 0)
    return pl.pallas_call(
        _final_kernel,
        grid=(t // tm,),
        in_specs=[pl.BlockSpec((tm, d), row),
                  pl.BlockSpec((1, d), lambda i: (0, 0)),
                  pl.BlockSpec((1, 1, d), bidx),
                  pl.BlockSpec((1, 1, d), bidx)],
        out_specs=pl.BlockSpec((tm, d), row),
        out_shape=jax.ShapeDtypeStruct((t, d), F32),
        compiler_params=_cparams(("arbitrary",)),
        name="final_norm",
    )(x2, nw, sh, sc)


def _moe_layer(x2, nw, sh, sc, g2, wr, wg, wu, wd, *, seq, tm_e):
    t, d = x2.shape
    wr_pad = jnp.zeros((d, LANES), F32).at[:, :N_EXPERTS].set(wr)
    h, meta, cnt = router(x2, nw, sh, sc, wr_pad, seq=seq)
    counts = cnt[0, :N_EXPERTS].astype(jnp.int32)
    tiles = (counts + tm_e - 1) // tm_e
    tile_end = jnp.cumsum(tiles)
    offs = (tile_end - tiles) * tm_e
    idx = meta[:, 0:2].astype(jnp.int32)
    rank = meta[:, 4:6].astype(jnp.int32)
    dest = (offs[idx] + rank).reshape(-1)
    n_tiles = -(-(2 * t) // tm_e) + N_EXPERTS
    p = n_tiles * tm_e
    tid = jnp.arange(n_tiles, dtype=jnp.int32)
    tile_e = jnp.minimum(jnp.sum(tid[:, None] >= tile_end[None, :], axis=1), N_EXPERTS - 1)
    tile_v = (tid < tile_end[-1]).astype(jnp.int32)
    tile_e = jnp.where(tile_v == 1, tile_e, tile_e[jnp.maximum(tile_end[-1] - 1, 0)]).astype(jnp.int32)
    entry = jnp.zeros((p,), jnp.int32).at[dest].set(
        jnp.arange(1, 2 * t + 1, dtype=jnp.int32), unique_indices=True, mode="promise_in_bounds") - 1
    is_pad = entry < 0
    pad_rank = jnp.cumsum(is_pad.astype(jnp.int32)) - 1
    src = jnp.where(is_pad, 0, entry // 2)
    dst = jnp.where(is_pad, 2 * t + pad_rank, (entry % 2) * t + entry // 2)
    ys = experts(tile_e, tile_v, src.reshape(n_tiles, 1, tm_e), dst.reshape(n_tiles, 1, tm_e),
                 h, wg, wu, wd, tm=tm_e)
    return combine(x2, meta, g2, ys, seq=seq)


def kernel(x, c, ada_w, ada_b, norm1_w, norm2_w, w_in, conv_w, a_log, dt_bias, dn_norm_w, gm_ln_w, gm_ln_b, gm_spatial_w, gm_spatial_b, w_out, ffn_w_gate, ffn_w_up, ffn_w_down, moe_router, moe_w_gate, moe_w_up, moe_w_down, final_ada_w, final_ada_b, final_norm_w):
    bsz, seq, d = x.shape
    depth = ada_w.shape[0]
    t = bsz * seq
    assert d == D_MODEL and seq % 1024 == 0
    x2 = x.reshape(t, d)

    mods = ada_mod(c, ada_w, ada_b[:, None, :])
    fmod = ada_mod(c, final_ada_w[None], final_ada_b[None, None, :])[0]

    w_in_p = jnp.concatenate(
        [w_in[:, :, 0:2048], w_in[:, :, 2056:3080], w_in[:, :, 2048:2056],
         jnp.zeros((depth, d, LANES - 8), F32)], axis=-1).astype(BF16)
    w_out_b = w_out.astype(BF16)
    causal = jnp.tril(jnp.ones((GM_CHUNK, GM_CHUNK), dtype=bool))
    ws = jnp.where(causal, gm_spatial_w, 0.0).astype(BF16)
    bs = jnp.broadcast_to(gm_spatial_b[..., None], (depth, GM_GROUPS, GM_CHUNK, LANES))
    hp = jnp.zeros((depth, 8, LANES), F32)
    hp = hp.at[:, 0, 4:8].set(-jnp.exp(a_log)).at[:, 1, 4:8].set(dt_bias)
    ffn_gu = [pack_gate_up(ffn_w_gate[j].astype(BF16), ffn_w_up[j].astype(BF16), FFN_TF)
              for j in range(ffn_w_gate.shape[0])]
    ffn_d = ffn_w_down.astype(BF16)
    moe_g, moe_u, moe_d = moe_w_gate.astype(BF16), moe_w_up.astype(BF16), moe_w_down.astype(BF16)

    for i in range(depth):
        mod = mods[i].reshape(bsz, 6, 1, d)
        sh1, sc1, g1, sh2, sc2, g2 = (mod[:, k] for k in range(6))
        q, kb, vb, kt, zs, gc, gct, ogm = inproj(
            x2, norm1_w[i][None], sh1, sc1, w_in_p[i], conv_w[i], hp[i],
            gm_ln_w[i][None], gm_ln_b[i][None], ws[i], bs[i], seq=seq)
        odn = deltanet(q, kb, vb, kt, zs, gc, gct, dn_norm_w[i][None], batch=bsz, seq=seq)
        x2 = outproj(odn, ogm, w_out_b[i], x2, g1, seq=seq)
        j = i // 2
        if i % 2 == 0:
            x2 = ffn(x2, norm2_w[i][None], sh2, sc2, g2, ffn_gu[j], ffn_d[j], seq=seq)
        else:
            x2 = _moe_layer(x2, norm2_w[i][None], sh2, sc2, g2, moe_router[j],
                            moe_g[j], moe_u[j], moe_d[j], seq=seq, tm_e=896)
    fm = fmod.reshape(bsz, 2, 1, d)
    out = final_norm(x2, final_norm_w[None], fm[:, 0], fm[:, 1], seq=seq)
    return out.reshape(bsz, seq, d)
```

```python
import functools
import math

import jax
import jax.numpy as jnp
from jax import lax
from jax.experimental import pallas as pl
from jax.experimental.pallas import tpu as pltpu

F32 = jnp.float32
BF16 = jnp.bfloat16

EPS = 1e-6
D_MODEL = 1024
DN_HEADS = 4
HEAD_DIM = 128
DN_WIDTH = DN_HEADS * HEAD_DIM
CONV_WIDTH = 4
GM_GROUPS = 4
GM_WIDTH = 512
GM_CHUNK = 128
DN_CHUNK = 128
N_EXPERTS = 8
LANES = 128
ROW_TILES = D_MODEL // LANES
C_QKV, C_Z, C_GU, C_GV, C_BA, C_END = 0, 1536, 2048, 2560, 3072, 3200

VMEM_LIMIT = 56 * 1024 * 1024


def _cparams(sem):
    return pltpu.CompilerParams(dimension_semantics=sem, vmem_limit_bytes=VMEM_LIMIT)


def _sigmoid(x):
    return 0.5 + 0.5 * jnp.tanh(0.5 * x)


def _silu(x):
    return x * _sigmoid(x)


def _gelu_tanh(x):
    return 0.5 * x * (1.0 + jnp.tanh(math.sqrt(2.0 / math.pi) * (x + 0.044715 * (x * x * x))))


def _softplus(x):
    return jnp.maximum(x, 0.0) + jnp.log(1.0 + jnp.exp(-jnp.abs(x)))


def _split_hi_lo(x):
    hi = x.astype(BF16)
    lo = (x - hi.astype(F32)).astype(BF16)
    return hi, lo


def _dot(a, b):
    return jnp.dot(a, b, preferred_element_type=F32)


def _ada_kernel(c_ref, w_ref, b_ref, o_ref):
    c = c_ref[...]
    ca = _silu(c)
    c_hi, c_lo = _split_hi_lo(ca)
    w = w_ref[0]
    w_hi, w_lo = _split_hi_lo(w)
    o_ref[0] = _dot(c_hi, w_hi) + _dot(c_lo, w_hi) + _dot(c_hi, w_lo) + b_ref[0]


def ada_mod(c, w, b, tn=2048):
    nl, d, n = w.shape
    bsz = c.shape[0]
    return pl.pallas_call(
        _ada_kernel,
        grid=(nl, n // tn),
        in_specs=[pl.BlockSpec((bsz, d), lambda l, j: (0, 0)),
                  pl.BlockSpec((1, d, tn), lambda l, j: (l, 0, j)),
                  pl.BlockSpec((1, 1, tn), lambda l, j: (l, 0, j))],
        out_specs=pl.BlockSpec((1, bsz, tn), lambda l, j: (l, 0, j)),
        out_shape=jax.ShapeDtypeStruct((nl, bsz, n), F32),
        compiler_params=_cparams(("arbitrary", "arbitrary")),
        name="ada_mod",
    )(c, w, b)


def _token_rows(t):
    return pl.ds(pl.multiple_of(t * ROW_TILES, ROW_TILES), ROW_TILES)


def _norm_mod(x, nw, sh, sc):
    ms = jnp.mean(x * x, axis=-1, keepdims=True)
    return (x * lax.rsqrt(ms + EPS) * nw) * (1.0 + sc) + sh


def _inproj_kernel(x_ref, nw_ref, sh_ref, sc_ref, w_ref, cw_ref, hp_ref, lnw_ref, lnb_ref,
                   ws_ref, bs_ref,
                   q_ref, kb_ref, vb_ref, kt_ref, zs_ref, gc_ref, gct_ref, ogm_ref,
                   ext_ref, *, tm, tiles_per_seq):
    i = pl.program_id(0)
    x = x_ref[...]
    hb = _norm_mod(x, nw_ref[...], sh_ref[0], sc_ref[0]).astype(BF16)

    @pl.when((i % tiles_per_seq) == 0)
    def _():
        ext_ref[0:8, :] = jnp.zeros((8, 3 * DN_WIDTH), F32)

    @pl.when((i % tiles_per_seq) != 0)
    def _():
        ext_ref[0:8, :] = ext_ref[tm:tm + 8, :]

    ext_ref[8:tm + 8, :] = _dot(hb, w_ref[:, C_QKV:C_Z])
    cw = cw_ref[...]
    conv = ext_ref[8:tm + 8, :] * cw[CONV_WIDTH - 1:CONV_WIDTH, :]
    for s in range(1, CONV_WIDTH):
        conv = conv + ext_ref[8 - s:tm + 8 - s, :] * cw[CONV_WIDTH - 1 - s:CONV_WIDTH - s, :]
    act = _silu(conv)

    ba = _dot(hb, w_ref[:, C_BA:C_END])
    beta = _sigmoid(ba)
    hp = hp_ref[...]
    g = hp[0:1, :] * _softplus(ba + hp[1:2, :])
    rowc = lax.broadcasted_iota(jnp.int32, (tm, LANES), 0) % DN_CHUNK
    gc = g
    s = 1
    while s < DN_CHUNK:
        gc = gc + jnp.where(rowc >= s, pltpu.roll(gc, s, 0), 0.0)
        s *= 2
    gc_ref[...] = gc
    gct = gc.T
    for j in range(tm // DN_CHUNK):
        gct_ref[j] = gct[0:8, j * DN_CHUNK:(j + 1) * DN_CHUNK]

    kn_all = []
    for h in range(DN_HEADS):
        lo, hi = h * HEAD_DIM, (h + 1) * HEAD_DIM
        qh = act[:, lo:hi]
        kh = act[:, DN_WIDTH + lo:DN_WIDTH + hi]
        vh = act[:, 2 * DN_WIDTH + lo:2 * DN_WIDTH + hi]
        qn = qh * lax.rsqrt(jnp.sum(qh * qh, axis=-1, keepdims=True) + EPS) * (HEAD_DIM ** -0.5)
        kn = kh * lax.rsqrt(jnp.sum(kh * kh, axis=-1, keepdims=True) + EPS)
        bh = beta[:, h:h + 1]
        q_ref[:, lo:hi] = qn.astype(BF16)
        kb_ref[:, lo:hi] = (kn * bh).astype(BF16)
        vb_ref[:, lo:hi] = (vh * bh).astype(BF16)
        kn_all.append(kn)
    kt = jnp.concatenate(kn_all, axis=1).T
    for j in range(tm // DN_CHUNK):
        kt_ref[j] = kt[:, j * DN_CHUNK:(j + 1) * DN_CHUNK].astype(BF16)

    zs_ref[...] = _silu(_dot(hb, w_ref[:, C_Z:C_GU])).astype(BF16)

    gu = _gelu_tanh(_dot(hb, w_ref[:, C_GU:C_GV]))
    gv = _gelu_tanh(_dot(hb, w_ref[:, C_GV:C_BA]))
    mu = jnp.mean(gv, axis=-1, keepdims=True)
    gvc = gv - mu
    var = jnp.mean(gvc * gvc, axis=-1, keepdims=True)
    gvn = (gvc * lax.rsqrt(var + EPS) * lnw_ref[...] + lnb_ref[...]).astype(BF16)
    for j in range(tm // GM_CHUNK):
        r0, r1 = j * GM_CHUNK, (j + 1) * GM_CHUNK
        for gi in range(GM_GROUPS):
            c0, c1 = gi * LANES, (gi + 1) * LANES
            sp = _dot(ws_ref[gi], gvn[r0:r1, c0:c1]) + bs_ref[gi]
            ogm_ref[r0:r1, c0:c1] = (gu[r0:r1, c0:c1] * sp).astype(BF16)


def inproj(x2, nw, sh, sc, w, cw, hp, lnw, lnb, ws, bs, *, seq, tm=512):
    t, d = x2.shape
    nt = t // tm
    tiles_per_seq = seq // tm
    nc = tm // DN_CHUNK
    row = lambda i: (i, 0)
    const2 = lambda i: (0, 0)
    const3 = lambda i: (0, 0, 0)
    bidx = lambda i: (i // tiles_per_seq, 0, 0)
    out_shape = [
        jax.ShapeDtypeStruct((t, DN_WIDTH), BF16),
        jax.ShapeDtypeStruct((t, DN_WIDTH), BF16),
        jax.ShapeDtypeStruct((t, DN_WIDTH), BF16),
        jax.ShapeDtypeStruct((t // DN_CHUNK, DN_WIDTH, DN_CHUNK), BF16),
        jax.ShapeDtypeStruct((t, DN_WIDTH), BF16),
        jax.ShapeDtypeStruct((t, LANES), F32),
        jax.ShapeDtypeStruct((t // DN_CHUNK, 8, DN_CHUNK), F32),
        jax.ShapeDtypeStruct((t, GM_WIDTH), BF16),
    ]
    out_specs = [
        pl.BlockSpec((tm, DN_WIDTH), row),
        pl.BlockSpec((tm, DN_WIDTH), row),
        pl.BlockSpec((tm, DN_WIDTH), row),
        pl.BlockSpec((nc, DN_WIDTH, DN_CHUNK), lambda i: (i, 0, 0)),
        pl.BlockSpec((tm, DN_WIDTH), row),
        pl.BlockSpec((tm, LANES), row),
        pl.BlockSpec((nc, 8, DN_CHUNK), lambda i: (i, 0, 0)),
        pl.BlockSpec((tm, GM_WIDTH), row),
    ]
    in_specs = [
        pl.BlockSpec((tm, d), row),
        pl.BlockSpec((1, d), const2),
        pl.BlockSpec((1, 1, d), bidx),
        pl.BlockSpec((1, 1, d), bidx),
        pl.BlockSpec((d, C_END), const2),
        pl.BlockSpec((CONV_WIDTH, 3 * DN_WIDTH), const2),
        pl.BlockSpec((8, LANES), const2),
        pl.BlockSpec((1, GM_WIDTH), const2),
        pl.BlockSpec((1, GM_WIDTH), const2),
        pl.BlockSpec((GM_GROUPS, GM_CHUNK, GM_CHUNK), const3),
        pl.BlockSpec((GM_GROUPS, GM_CHUNK, LANES), const3),
    ]
    return pl.pallas_call(
        functools.partial(_inproj_kernel, tm=tm, tiles_per_seq=tiles_per_seq),
        grid=(nt,),
        in_specs=in_specs,
        out_specs=out_specs,
        out_shape=out_shape,
        scratch_shapes=[pltpu.VMEM((tm + 8, 3 * DN_WIDTH), F32)],
        compiler_params=_cparams(("arbitrary",)),
        name="inproj",
    )(x2, nw, sh, sc, w, cw, hp, lnw, lnb, ws, bs)


INV_BASE = 8


def _unit_lower_inverse(a_list, ri, ci):
    c = a_list[0].shape[0]
    same = (ri // INV_BASE) == (ci // INV_BASE)
    eye = jnp.where(ri == ci, 1.0, 0.0)
    ad = [jnp.where(same, a, 0.0) for a in a_list]
    x = [eye - a for a in ad]
    pb = [a.astype(BF16) for a in ad]
    p = [_dot(b, b) for b in pb]
    n = 2
    while True:
        pb = [m.astype(BF16) for m in p]
        x = [xi + _dot(xi.astype(BF16), b) for xi, b in zip(x, pb)]
        n *= 2
        if n >= INV_BASE:
            break
        p = [_dot(b, b) for b in pb]
    s = INV_BASE
    while s < c:
        off = jnp.logical_and((ri // (2 * s)) == (ci // (2 * s)), (ri // s) != (ci // s))
        xb = [xi.astype(BF16) for xi in x]
        y = [_dot(b, jnp.where(off, a, 0.0).astype(BF16)).astype(BF16) for b, a in zip(xb, a_list)]
        x = [xi - _dot(yi, b) for xi, yi, b in zip(x, y, xb)]
        s *= 2
    return x


def _deltanet_kernel(q_ref, kb_ref, vb_ref, kt_ref, zs_ref, gc_ref, gct_ref, nw_ref,
                     o_ref, s_ref, u_ref, w_ref, attn_ref, *, tl, unroll):
    l = pl.program_id(1)

    @pl.when(l == 0)
    def _():
        s_ref[...] = jnp.zeros_like(s_ref)

    c = DN_CHUNK
    ri = lax.broadcasted_iota(jnp.int32, (c, c), 0)
    ci = lax.broadcasted_iota(jnp.int32, (c, c), 1)
    tri = ri >= ci
    strict = ri > ci
    nw = nw_ref[...]
    heads = [(bb, h, h * HEAD_DIM, (h + 1) * HEAD_DIM)
             for bb in range(DN_PAIR) for h in range(DN_HEADS)]

    def prep(it, carry):
        items = []
        for sub in range(unroll):
            ch = it * unroll + sub
            r0 = pl.multiple_of(ch * c, c)
            for bb in range(DN_PAIR):
                gc = gc_ref[bb, pl.ds(r0, c), :]
                gct = gct_ref[bb, ch]
                for h in range(DN_HEADS):
                    items.append((bb, ch, r0, h, h * HEAD_DIM, (h + 1) * HEAD_DIM,
                                  jnp.broadcast_to(gc[:, 4 + h:5 + h], (c, c)),
                                  jnp.broadcast_to(gct[4 + h:5 + h, :], (c, c))))
        kts = [kt_ref[bb, ch, lo:hi, :] for bb, ch, r0, h, lo, hi, _, _ in items]
        kbs = [kb_ref[bb, pl.ds(r0, c), lo:hi] for bb, ch, r0, h, lo, hi, _, _ in items]
        decay = [jnp.where(tri, jnp.exp(jnp.minimum(gcol - grow, 0.0)), 0.0)
                 for *_, gcol, grow in items]
        a = [jnp.where(strict, _dot(kb, kt) * dc, 0.0) for kb, kt, dc in zip(kbs, kts, decay)]
        for (bb, ch, r0, h, lo, hi, _, _), kt, dc in zip(items, kts, decay):
            attn_ref[bb, ch, h] = (_dot(q_ref[bb, pl.ds(r0, c), lo:hi], kt) * dc).astype(BF16)
        tb = [t.astype(BF16) for t in _unit_lower_inverse(a, ri, ci)]
        for (bb, ch, r0, h, lo, hi, gcol, _), t, kb in zip(items, tb, kbs):
            u_ref[bb, pl.ds(r0, c), lo:hi] = _dot(t, vb_ref[bb, pl.ds(r0, c), lo:hi])
            w_ref[bb, pl.ds(r0, c), lo:hi] = _dot(
                t, (kb.astype(F32) * jnp.exp(gcol)).astype(BF16)).astype(BF16)
        return carry

    lax.fori_loop(0, tl // (c * unroll), prep, 0)

    def step(ch, carry):
        r0 = pl.multiple_of(ch * c, c)
        gcs = [gc_ref[bb, pl.ds(r0, c), :] for bb in range(DN_PAIR)]
        gcol = [jnp.broadcast_to(gcs[bb][:, 4 + h:5 + h], (c, c)) for bb, h, lo, hi in heads]
        glast = [jnp.broadcast_to(g[c - 1:c, :], (c, c)) for g in gcol]
        q = [q_ref[bb, pl.ds(r0, c), lo:hi] for bb, h, lo, hi in heads]
        sb = [s_ref[bb * DN_HEADS + h].astype(BF16) for bb, h, lo, hi in heads]
        ws = [_dot(w_ref[bb, pl.ds(r0, c), lo:hi], s) for (bb, h, lo, hi), s in zip(heads, sb)]
        qs = [_dot(qh, s) for qh, s in zip(q, sb)]
        v_new = [u_ref[bb, pl.ds(r0, c), lo:hi] - w for (bb, h, lo, hi), w in zip(heads, ws)]
        av = [_dot(attn_ref[bb, ch, h], v.astype(BF16))
              for (bb, h, lo, hi), v in zip(heads, v_new)]
        kv = [_dot(kt_ref[bb, ch, lo:hi, :], (v * jnp.exp(gl - g)).astype(BF16))
              for (bb, h, lo, hi), v, gl, g in zip(heads, v_new, glast, gcol)]
        for (bb, h, lo, hi), upd, gl in zip(heads, kv, glast):
            s_ref[bb * DN_HEADS + h] = s_ref[bb * DN_HEADS + h] * jnp.exp(gl) + upd
        for (bb, h, lo, hi), qsh, avh, g in zip(heads, qs, av, gcol):
            o = qsh * jnp.exp(g) + avh
            ms = jnp.mean(o * o, axis=-1, keepdims=True)
            on = o * lax.rsqrt(ms + EPS) * nw
            zs = zs_ref[bb, pl.ds(r0, c), lo:hi].astype(F32)
            o_ref[bb, pl.ds(r0, c), lo:hi] = (on * zs).astype(BF16)
        return carry

    lax.fori_loop(0, tl // c, step, 0)


DN_PAIR = 2


def deltanet(q, kb, vb, kt, zs, gc, gct, nw, *, batch, seq, tl=1024):
    t = q.shape[0]
    nl = seq // tl
    nc = tl // DN_CHUNK
    assert batch % DN_PAIR == 0
    seqs = seq // DN_CHUNK
    rows = lambda a: a.reshape(batch, seq, a.shape[-1])
    chunks = lambda a: a.reshape(batch, seqs, a.shape[-2], a.shape[-1])
    blk = lambda w: pl.BlockSpec((DN_PAIR, tl, w), lambda b, l: (b, l, 0))
    blk4 = lambda r: pl.BlockSpec((DN_PAIR, nc, r, DN_CHUNK), lambda b, l: (b, l, 0, 0))
    out = pl.pallas_call(
        functools.partial(_deltanet_kernel, tl=tl, unroll=2),
        grid=(batch // DN_PAIR, nl),
        in_specs=[blk(DN_WIDTH), blk(DN_WIDTH), blk(DN_WIDTH), blk4(DN_WIDTH), blk(DN_WIDTH),
                  blk(LANES), blk4(8),
                  pl.BlockSpec((1, HEAD_DIM), lambda b, l: (0, 0))],
        out_specs=blk(DN_WIDTH),
        out_shape=jax.ShapeDtypeStruct((batch, seq, DN_WIDTH), BF16),
        scratch_shapes=[pltpu.VMEM((DN_PAIR * DN_HEADS, HEAD_DIM, HEAD_DIM), F32),
                        pltpu.VMEM((DN_PAIR, tl, DN_WIDTH), F32),
                        pltpu.VMEM((DN_PAIR, tl, DN_WIDTH), BF16),
                        pltpu.VMEM((DN_PAIR, nc, DN_HEADS, DN_CHUNK, DN_CHUNK), BF16)],
        compiler_params=_cparams(("arbitrary", "arbitrary")),
        name="deltanet",
    )(rows(q), rows(kb), rows(vb), chunks(kt), rows(zs), rows(gc), chunks(gct), nw)
    return out.reshape(t, DN_WIDTH)


def _outproj_kernel(odn_ref, ogm_ref, w_ref, x_ref, g_ref, o_ref):
    mix = _dot(odn_ref[...], w_ref[0:DN_WIDTH, :]) + _dot(ogm_ref[...], w_ref[DN_WIDTH:, :])
    o_ref[...] = x_ref[...] + g_ref[0] * mix


def outproj(odn, ogm, w, x2, g1, *, seq, tm=1024):
    t, d = x2.shape
    tiles_per_seq = seq // tm
    row = lambda i: (i, 0)
    return pl.pallas_call(
        _outproj_kernel,
        grid=(t // tm,),
        in_specs=[pl.BlockSpec((tm, DN_WIDTH), row),
                  pl.BlockSpec((tm, GM_WIDTH), row),
                  pl.BlockSpec((d, d), lambda i: (0, 0)),
                  pl.BlockSpec((tm, d), row),
                  pl.BlockSpec((1, 1, d), lambda i: (i // tiles_per_seq, 0, 0))],
        out_specs=pl.BlockSpec((tm, d), row),
        out_shape=jax.ShapeDtypeStruct((t, d), F32),
        compiler_params=_cparams(("arbitrary",)),
        name="outproj",
    )(odn, ogm, w, x2, g1)


def _ffn_kernel(x_ref, nw_ref, sh_ref, sc_ref, g_ref, wgu_ref, wd_ref, o_ref,
                hb_ref, acc_ref, *, tf):
    f = pl.program_id(1)

    @pl.when(f == 0)
    def _():
        hb_ref[...] = _norm_mod(x_ref[...], nw_ref[...], sh_ref[0], sc_ref[0]).astype(BF16)
        acc_ref[...] = jnp.zeros_like(acc_ref)

    gu = _dot(hb_ref[...], wgu_ref[0])
    acc_ref[...] += _dot((_silu(gu[:, :tf]) * gu[:, tf:]).astype(BF16), wd_ref[...])

    @pl.when(f == pl.num_programs(1) - 1)
    def _():
        o_ref[...] = x_ref[...] + g_ref[0] * acc_ref[...]


FFN_TF = 1408


def pack_gate_up(wg, wu, tf):
    d, dff = wg.shape
    nf = dff // tf
    gu = jnp.concatenate([wg.reshape(d, nf, tf), wu.reshape(d, nf, tf)], axis=-1)
    return jnp.swapaxes(gu, 0, 1)


def ffn(x2, nw, sh, sc, g2, wgu, wd, *, seq, tm=512):
    t, d = x2.shape
    nf, _, tf2 = wgu.shape
    tf = tf2 // 2
    tiles_per_seq = seq // tm
    row = lambda i, f: (i, 0)
    bidx = lambda i, f: (i // tiles_per_seq, 0, 0)
    return pl.pallas_call(
        functools.partial(_ffn_kernel, tf=tf),
        grid=(t // tm, nf),
        in_specs=[pl.BlockSpec((tm, d), row),
                  pl.BlockSpec((1, d), lambda i, f: (0, 0)),
                  pl.BlockSpec((1, 1, d), bidx),
                  pl.BlockSpec((1, 1, d), bidx),
                  pl.BlockSpec((1, 1, d), bidx),
                  pl.BlockSpec((1, d, tf2), lambda i, f: (f, 0, 0)),
                  pl.BlockSpec((tf, d), lambda i, f: (f, 0))],
        out_specs=pl.BlockSpec((tm, d), row),
        out_shape=jax.ShapeDtypeStruct((t, d), F32),
        scratch_shapes=[pltpu.VMEM((tm, d), BF16), pltpu.VMEM((tm, d), F32)],
        compiler_params=_cparams(("arbitrary", "arbitrary")),
        name="ffn",
    )(x2, nw, sh, sc, g2, wgu, wd)


def _router_kernel(x_ref, nw_ref, sh_ref, sc_ref, wr_ref, h_ref, meta_ref, cnt_ref,
                   carry_ref, *, tm):
    i = pl.program_id(0)

    @pl.when(i == 0)
    def _():
        carry_ref[...] = jnp.zeros_like(carry_ref)

    h = _norm_mod(x_ref[...], nw_ref[...], sh_ref[0], sc_ref[0])
    for k in range(ROW_TILES):
        h_ref[pl.ds(k, tm, stride=ROW_TILES), :] = h[:, k * LANES:(k + 1) * LANES]
    h_hi, h_lo = _split_hi_lo(h)
    w_hi, w_lo = _split_hi_lo(wr_ref[...])
    logits = _dot(h_hi, w_hi) + _dot(h_lo, w_hi) + _dot(h_hi, w_lo)
    lane = lax.broadcasted_iota(jnp.int32, (tm, LANES), 1).astype(F32)
    neg = jnp.float32(-jnp.inf)
    lg = jnp.where(lane < N_EXPERTS, logits, neg)
    m1 = jnp.max(lg, axis=-1, keepdims=True)
    i1 = jnp.min(jnp.where(lg == m1, lane, float(LANES)), axis=-1, keepdims=True)
    lg2 = jnp.where(lane == i1, neg, lg)
    m2 = jnp.max(lg2, axis=-1, keepdims=True)
    i2 = jnp.min(jnp.where(lg2 == m2, lane, float(LANES)), axis=-1, keepdims=True)
    e2 = jnp.exp(m2 - m1)
    w1 = 1.0 / (1.0 + e2)
    w2 = e2 / (1.0 + e2)
    oh1 = lane == i1
    oh2 = lane == i2
    onehot = jnp.where(oh1 | oh2, 1.0, 0.0)
    ri = lax.broadcasted_iota(jnp.int32, (tm, tm), 0)
    ci = lax.broadcasted_iota(jnp.int32, (tm, tm), 1)
    lower = jnp.where(ri > ci, 1.0, 0.0).astype(BF16)
    before = _dot(lower, onehot.astype(BF16)) + carry_ref[0:1, :]
    r1 = jnp.sum(jnp.where(oh1, before, 0.0), axis=-1, keepdims=True)
    r2 = jnp.sum(jnp.where(oh2, before, 0.0), axis=-1, keepdims=True)
    total = carry_ref[0:1, :] + jnp.sum(onehot, axis=0, keepdims=True)
    carry_ref[...] = jnp.broadcast_to(total, carry_ref.shape)
    cnt_ref[...] = jnp.broadcast_to(total, cnt_ref.shape)
    meta = jnp.where(lane == 0, i1, 0.0)
    meta = jnp.where(lane == 1, i2, meta)
    meta = jnp.where(lane == 2, w1, meta)
    meta = jnp.where(lane == 3, w2, meta)
    meta = jnp.where(lane == 4, r1, meta)
    meta = jnp.where(lane == 5, r2, meta)
    meta_ref[...] = meta


def router(x2, nw, sh, sc, wr, *, seq, tm=512):
    t, d = x2.shape
    tiles_per_seq = seq // tm
    row = lambda i: (i, 0)
    bidx = lambda i: (i // tiles_per_seq, 0, 0)
    return pl.pallas_call(
        functools.partial(_router_kernel, tm=tm),
        grid=(t // tm,),
        in_specs=[pl.BlockSpec((tm, d), row),
                  pl.BlockSpec((1, d), lambda i: (0, 0)),
                  pl.BlockSpec((1, 1, d), bidx),
                  pl.BlockSpec((1, 1, d), bidx),
                  pl.BlockSpec((d, LANES), lambda i: (0, 0))],
        out_specs=[pl.BlockSpec((tm * ROW_TILES, LANES), row),
                   pl.BlockSpec((tm, LANES), row),
                   pl.BlockSpec((8, LANES), lambda i: (0, 0))],
        out_shape=[jax.ShapeDtypeStruct((t * ROW_TILES, LANES), F32),
                   jax.ShapeDtypeStruct((t, LANES), F32),
                   jax.ShapeDtypeStruct((8, LANES), F32)],
        scratch_shapes=[pltpu.VMEM((8, LANES), F32)],
        compiler_params=_cparams(("arbitrary",)),
        name="router",
    )(x2, nw, sh, sc, wr)


def _experts_kernel(te_ref, tv_ref, src_cur, src_next, dst_prev, dst_cur, h_hbm,
                    wg_ref, wu_ref, wd_ref, ys_hbm,
                    xbuf, xb_ref, acc_ref, obuf, gate_ref, act_ref, gsem, ssem, *, tm, rows):
    i = pl.program_id(0)
    f = pl.program_id(1)
    last_i = pl.num_programs(0) - 1
    last_f = pl.num_programs(1) - 1
    valid = tv_ref[i] == 1

    def gather_row(idx_ref, r):
        t = idx_ref[0, 0, r]
        pltpu.make_async_copy(h_hbm.at[_token_rows(t), :], xbuf.at[_token_rows(r), :], gsem).start()

    def scatter_row(idx_ref, r):
        d = idx_ref[0, 0, r]
        pltpu.make_async_copy(obuf.at[_token_rows(r), :], ys_hbm.at[_token_rows(d), :], ssem).start()

    def wait_gather():
        pltpu.make_async_copy(h_hbm.at[pl.ds(0, tm * ROW_TILES), :], xbuf, gsem).wait()

    def wait_scatter():
        pltpu.make_async_copy(obuf, ys_hbm.at[pl.ds(0, tm * ROW_TILES), :], ssem).wait()

    @pl.when(f == 0)
    def _():
        @pl.when(i == 0)
        def _():
            obuf[...] = jnp.zeros_like(obuf)

            def first(r, carry):
                gather_row(src_cur, r)
                return carry

            lax.fori_loop(0, tm, first, 0)

        wait_gather()
        for k in range(ROW_TILES):
            xb_ref[:, k * LANES:(k + 1) * LANES] = xbuf[pl.ds(k, tm, stride=ROW_TILES), :].astype(BF16)
        acc_ref[...] = jnp.zeros_like(acc_ref)

    def move_rows(lo, hi):
        for j in range(lo, hi):
            gather_row(src_next, f * rows + j)
            scatter_row(dst_prev, f * rows + j)

    cut1, cut2 = rows // 4, rows // 2

    def once(block):
        def body(_, carry):
            block()
            return carry

        lax.fori_loop(0, tv_ref[i], body, 0)

    def block_gate():
        move_rows(0, cut1)
        gate_ref[...] = _dot(xb_ref[...], wg_ref[0])

    def block_up():
        move_rows(cut1, cut2)
        up = _dot(xb_ref[...], wu_ref[0])
        act_ref[...] = (_silu(gate_ref[...]) * up).astype(BF16)

    def block_down():
        move_rows(cut2, rows)
        acc_ref[...] += _dot(act_ref[...], wd_ref[0])

    once(block_gate)
    once(block_up)
    once(block_down)

    @pl.when(jnp.logical_not(valid))
    def _():
        move_rows(0, rows)

    @pl.when(f == last_f)
    def _():
        wait_scatter()
        for k in range(ROW_TILES):
            obuf[pl.ds(k, tm, stride=ROW_TILES), :] = acc_ref[:, k * LANES:(k + 1) * LANES]

        @pl.when(i == last_i)
        def _():
            def final(r, carry):
                scatter_row(dst_cur, r)
                return carry

            lax.fori_loop(0, tm, final, 0)
            wait_scatter()
            wait_gather()


def experts(tile_e, tile_v, src3, dst3, h, wg, wu, wd, *, tm, tf=1792):
    nt = src3.shape[0]
    d = D_MODEL
    flat = (tm * ROW_TILES, LANES)
    dff = wg.shape[2]
    nf = dff // tf
    rows = tm // nf
    assert rows * nf == tm

    def fsel(i, f, tv):
        return jnp.where(tv[i] == 1, f, nf - 1)

    smem = functools.partial(pl.BlockSpec, (1, 1, tm), memory_space=pltpu.SMEM)
    grid_spec = pltpu.PrefetchScalarGridSpec(
        num_scalar_prefetch=2,
        grid=(nt, nf),
        in_specs=[smem(lambda i, f, te, tv: (i, 0, 0)),
                  smem(lambda i, f, te, tv: (jnp.minimum(i + 1, nt - 1), 0, 0)),
                  smem(lambda i, f, te, tv: (jnp.maximum(i - 1, 0), 0, 0)),
                  smem(lambda i, f, te, tv: (i, 0, 0)),
                  pl.BlockSpec(memory_space=pl.ANY),
                  pl.BlockSpec((1, d, tf), lambda i, f, te, tv: (te[i], 0, fsel(i, f, tv))),
                  pl.BlockSpec((1, d, tf), lambda i, f, te, tv: (te[i], 0, fsel(i, f, tv))),
                  pl.BlockSpec((1, tf, d), lambda i, f, te, tv: (te[i], fsel(i, f, tv), 0))],
        out_specs=pl.BlockSpec(memory_space=pl.ANY),
        scratch_shapes=[pltpu.VMEM(flat, F32), pltpu.VMEM((tm, d), BF16),
                        pltpu.VMEM((tm, d), F32), pltpu.VMEM(flat, F32),
                        pltpu.VMEM((tm, tf), F32), pltpu.VMEM((tm, tf), BF16),
                        pltpu.SemaphoreType.DMA(()), pltpu.SemaphoreType.DMA(())],
    )
    return pl.pallas_call(
        functools.partial(_experts_kernel, tm=tm, rows=rows),
        grid_spec=grid_spec,
        out_shape=jax.ShapeDtypeStruct((nt * tm * ROW_TILES, LANES), F32),
        compiler_params=_cparams(("arbitrary", "arbitrary")),
        name="experts",
    )(tile_e, tile_v, src3, src3, dst3, dst3, h, wg, wu, wd)


def _combine_kernel(x_ref, meta_ref, g_ref, y0_ref, y1_ref, o_ref, *, tm):
    meta = meta_ref[...]
    w0, w1 = meta[:, 2:3], meta[:, 3:4]
    for k in range(ROW_TILES):
        cols = slice(k * LANES, (k + 1) * LANES)
        rows = pl.ds(k, tm, stride=ROW_TILES)
        y = w0 * y0_ref[rows, :] + w1 * y1_ref[rows, :]
        o_ref[:, cols] = x_ref[:, cols] + g_ref[0, :, cols] * y


def combine(x2, meta, g2, ys, *, seq, tm=512):
    t, d = x2.shape
    tiles_per_seq = seq // tm
    nt = t // tm
    row = lambda i: (i, 0)
    return pl.pallas_call(
        functools.partial(_combine_kernel, tm=tm),
        grid=(nt,),
        in_specs=[pl.BlockSpec((tm, d), row),
                  pl.BlockSpec((tm, LANES), row),
                  pl.BlockSpec((1, 1, d), lambda i: (i // tiles_per_seq, 0, 0)),
                  pl.BlockSpec((tm * ROW_TILES, LANES), row),
                  pl.BlockSpec((tm * ROW_TILES, LANES), lambda i: (nt + i, 0))],
        out_specs=pl.BlockSpec((tm, d), row),
        out_shape=jax.ShapeDtypeStruct((t, d), F32),
        compiler_params=_cparams(("arbitrary",)),
        name="combine",
    )(x2, meta, g2, ys, ys)


def _final_kernel(x_ref, nw_ref, sh_ref, sc_ref, o_ref):
    o_ref[...] = _norm_mod(x_ref[...], nw_ref[...], sh_ref[0], sc_ref[0])


def final_norm(x2, nw, sh, sc, *, seq, tm=1024):
    t, d = x2.shape
    tiles_per_seq = seq // tm
    row = lambda i: (i, 0)
    bidx = lambda i: (i // tiles_per_seq, 0, 0)
    return pl.pallas_call(
        _final_kernel,
        grid=(t // tm,),
        in_specs=[pl.BlockSpec((tm, d), row),
                  pl.BlockSpec((1, d), lambda i: (0, 0)),
                  pl.BlockSpec((1, 1, d), bidx),
                  pl.BlockSpec((1, 1, d), bidx)],
        out_specs=pl.BlockSpec((tm, d), row),
        out_shape=jax.ShapeDtypeStruct((t, d), F32),
        compiler_params=_cparams(("arbitrary",)),
        name="final_norm",
    )(x2, nw, sh, sc)


def _moe_layer(x2, nw, sh, sc, g2, wr, wg, wu, wd, *, seq, tm_e):
    t, d = x2.shape
    wr_pad = jnp.zeros((d, LANES), F32).at[:, :N_EXPERTS].set(wr)
    h, meta, cnt = router(x2, nw, sh, sc, wr_pad, seq=seq)
    counts = cnt[0, :N_EXPERTS].astype(jnp.int32)
    tiles = (counts + tm_e - 1) // tm_e
    tile_end = jnp.cumsum(tiles)
    offs = (tile_end - tiles) * tm_e
    idx = meta[:, 0:2].astype(jnp.int32)
    rank = meta[:, 4:6].astype(jnp.int32)
    dest = (offs[idx] + rank).reshape(-1)
    n_tiles = -(-(2 * t) // tm_e) + N_EXPERTS
    p = n_tiles * tm_e
    tid = jnp.arange(n_tiles, dtype=jnp.int32)
    tile_e = jnp.minimum(jnp.sum(tid[:, None] >= tile_end[None, :], axis=1), N_EXPERTS - 1)
    tile_v = (tid < tile_end[-1]).astype(jnp.int32)
    tile_e = jnp.where(tile_v == 1, tile_e, tile_e[jnp.maximum(tile_end[-1] - 1, 0)]).astype(jnp.int32)
    entry = jnp.zeros((p,), jnp.int32).at[dest].set(
        jnp.arange(1, 2 * t + 1, dtype=jnp.int32), unique_indices=True, mode="promise_in_bounds") - 1
    is_pad = entry < 0
    pad_rank = jnp.cumsum(is_pad.astype(jnp.int32)) - 1
    src = jnp.where(is_pad, 0, entry // 2)
    dst = jnp.where(is_pad, 2 * t + pad_rank, (entry % 2) * t + entry // 2)
    ys = experts(tile_e, tile_v, src.reshape(n_tiles, 1, tm_e), dst.reshape(n_tiles, 1, tm_e),
                 h, wg, wu, wd, tm=tm_e)
    return combine(x2, meta, g2, ys, seq=seq)


def kernel(x, c, ada_w, ada_b, norm1_w, norm2_w, w_in, conv_w, a_log, dt_bias, dn_norm_w, gm_ln_w, gm_ln_b, gm_spatial_w, gm_spatial_b, w_out, ffn_w_gate, ffn_w_up, ffn_w_down, moe_router, moe_w_gate, moe_w_up, moe_w_down, final_ada_w, final_ada_b, final_norm_w):
    bsz, seq, d = x.shape
    depth = ada_w.shape[0]
    t = bsz * seq
    assert d == D_MODEL and seq % 1024 == 0
    x2 = x.reshape(t, d)

    mods = ada_mod(c, ada_w, ada_b[:, None, :])
    fmod = ada_mod(c, final_ada_w[None], final_ada_b[None, None, :])[0]

    w_in_p = jnp.concatenate(
        [w_in[:, :, 0:2048], w_in[:, :, 2056:3080], w_in[:, :, 2048:2056],
         jnp.zeros((depth, d, LANES - 8), F32)], axis=-1).astype(BF16)
    w_out_b = w_out.astype(BF16)
    causal = jnp.tril(jnp.ones((GM_CHUNK, GM_CHUNK), dtype=bool))
    ws = jnp.where(causal, gm_spatial_w, 0.0).astype(BF16)
    bs = jnp.broadcast_to(gm_spatial_b[..., None], (depth, GM_GROUPS, GM_CHUNK, LANES))
    hp = jnp.zeros((depth, 8, LANES), F32)
    hp = hp.at[:, 0, 4:8].set(-jnp.exp(a_log)).at[:, 1, 4:8].set(dt_bias)
    ffn_gu = [pack_gate_up(ffn_w_gate[j].astype(BF16), ffn_w_up[j].astype(BF16), FFN_TF)
              for j in range(ffn_w_gate.shape[0])]
    ffn_d = ffn_w_down.astype(BF16)
    moe_g, moe_u, moe_d = moe_w_gate.astype(BF16), moe_w_up.astype(BF16), moe_w_down.astype(BF16)

    for i in range(depth):
        mod = mods[i].reshape(bsz, 6, 1, d)
        sh1, sc1, g1, sh2, sc2, g2 = (mod[:, k] for k in range(6))
        q, kb, vb, kt, zs, gc, gct, ogm = inproj(
            x2, norm1_w[i][None], sh1, sc1, w_in_p[i], conv_w[i], hp[i],
            gm_ln_w[i][None], gm_ln_b[i][None], ws[i], bs[i], seq=seq)
        odn = deltanet(q, kb, vb, kt, zs, gc, gct, dn_norm_w[i][None], batch=bsz, seq=seq)
        x2 = outproj(odn, ogm, w_out_b[i], x2, g1, seq=seq)
        j = i // 2
        if i % 2 == 0:
            x2 = ffn(x2, norm2_w[i][None], sh2, sc2, g2, ffn_gu[j], ffn_d[j], seq=seq)
        else:
            x2 = _moe_layer(x2, norm2_w[i][None], sh2, sc2, g2, moe_router[j],
                            moe_g[j], moe_u[j], moe_d[j], seq=seq, tm_e=896)
    fm = fmod.reshape(bsz, 2, 1, d)
    out = final_norm(x2, final_norm_w[None], fm[:, 0], fm[:, 1], seq=seq)
    return out.reshape(bsz, seq, d)
```

```python
import functools
import math

import jax
import jax.numpy as jnp
from jax import lax
from jax.experimental import pallas as pl
from jax.experimental.pallas import tpu as pltpu

F32 = jnp.float32
BF16 = jnp.bfloat16

EPS = 1e-6
D_MODEL = 1024
DN_HEADS = 4
HEAD_DIM = 128
DN_WIDTH = DN_HEADS * HEAD_DIM
CONV_WIDTH = 4
GM_GROUPS = 4
GM_WIDTH = 512
GM_CHUNK = 128
DN_CHUNK = 128
N_EXPERTS = 8
LANES = 128
ROW_TILES = D_MODEL // LANES
C_QKV, C_Z, C_GU, C_GV, C_BA, C_END = 0, 1536, 2048, 2560, 3072, 3200

VMEM_LIMIT = 56 * 1024 * 1024


def _cparams(sem):
    return pltpu.CompilerParams(dimension_semantics=sem, vmem_limit_bytes=VMEM_LIMIT)


def _sigmoid(x):
    return 0.5 + 0.5 * jnp.tanh(0.5 * x)


def _silu(x):
    return x * _sigmoid(x)


def _gelu_tanh(x):
    return 0.5 * x * (1.0 + jnp.tanh(math.sqrt(2.0 / math.pi) * (x + 0.044715 * (x * x * x))))


def _softplus(x):
    return jnp.maximum(x, 0.0) + jnp.log(1.0 + jnp.exp(-jnp.abs(x)))


def _split_hi_lo(x):
    hi = x.astype(BF16)
    lo = (x - hi.astype(F32)).astype(BF16)
    return hi, lo


def _dot(a, b):
    return jnp.dot(a, b, preferred_element_type=F32)


def _ada_kernel(c_ref, w_ref, b_ref, o_ref):
    c = c_ref[...]
    ca = _silu(c)
    c_hi, c_lo = _split_hi_lo(ca)
    w = w_ref[0]
    w_hi, w_lo = _split_hi_lo(w)
    o_ref[0] = _dot(c_hi, w_hi) + _dot(c_lo, w_hi) + _dot(c_hi, w_lo) + b_ref[0]


def ada_mod(c, w, b, tn=2048):
    nl, d, n = w.shape
    bsz = c.shape[0]
    return pl.pallas_call(
        _ada_kernel,
        grid=(nl, n // tn),
        in_specs=[pl.BlockSpec((bsz, d), lambda l, j: (0, 0)),
                  pl.BlockSpec((1, d, tn), lambda l, j: (l, 0, j)),
                  pl.BlockSpec((1, 1, tn), lambda l, j: (l, 0, j))],
        out_specs=pl.BlockSpec((1, bsz, tn), lambda l, j: (l, 0, j)),
        out_shape=jax.ShapeDtypeStruct((nl, bsz, n), F32),
        compiler_params=_cparams(("arbitrary", "arbitrary")),
        name="ada_mod",
    )(c, w, b)


def _token_rows(t):
    return pl.ds(pl.multiple_of(t * ROW_TILES, ROW_TILES), ROW_TILES)


def _norm_mod(x, nw, sh, sc):
    ms = jnp.mean(x * x, axis=-1, keepdims=True)
    return (x * lax.rsqrt(ms + EPS) * nw) * (1.0 + sc) + sh


def _inproj_kernel(x_ref, nw_ref, sh_ref, sc_ref, w_ref, cw_ref, hp_ref, lnw_ref, lnb_ref,
                   ws_ref, bs_ref,
                   q_ref, kb_ref, vb_ref, kt_ref, zs_ref, gc_ref, gct_ref, ogm_ref,
                   ext_ref, *, tm, tiles_per_seq):
    i = pl.program_id(0)
    x = x_ref[...]
    hb = _norm_mod(x, nw_ref[...], sh_ref[0], sc_ref[0]).astype(BF16)

    @pl.when((i % tiles_per_seq) == 0)
    def _():
        ext_ref[0:8, :] = jnp.zeros((8, 3 * DN_WIDTH), F32)

    @pl.when((i % tiles_per_seq) != 0)
    def _():
        ext_ref[0:8, :] = ext_ref[tm:tm + 8, :]

    ext_ref[8:tm + 8, :] = _dot(hb, w_ref[:, C_QKV:C_Z])
    cw = cw_ref[...]
    conv = ext_ref[8:tm + 8, :] * cw[CONV_WIDTH - 1:CONV_WIDTH, :]
    for s in range(1, CONV_WIDTH):
        conv = conv + ext_ref[8 - s:tm + 8 - s, :] * cw[CONV_WIDTH - 1 - s:CONV_WIDTH - s, :]
    act = _silu(conv)

    ba = _dot(hb, w_ref[:, C_BA:C_END])
    beta = _sigmoid(ba)
    hp = hp_ref[...]
    g = hp[0:1, :] * _softplus(ba + hp[1:2, :])
    rowc = lax.broadcasted_iota(jnp.int32, (tm, LANES), 0) % DN_CHUNK
    gc = g
    s = 1
    while s < DN_CHUNK:
        gc = gc + jnp.where(rowc >= s, pltpu.roll(gc, s, 0), 0.0)
        s *= 2
    gc_ref[...] = gc
    gct = gc.T
    for j in range(tm // DN_CHUNK):
        gct_ref[j] = gct[0:8, j * DN_CHUNK:(j + 1) * DN_CHUNK]

    kn_all = []
    for h in range(DN_HEADS):
        lo, hi = h * HEAD_DIM, (h + 1) * HEAD_DIM
        qh = act[:, lo:hi]
        kh = act[:, DN_WIDTH + lo:DN_WIDTH + hi]
        vh = act[:, 2 * DN_WIDTH + lo:2 * DN_WIDTH + hi]
        qn = qh * lax.rsqrt(jnp.sum(qh * qh, axis=-1, keepdims=True) + EPS) * (HEAD_DIM ** -0.5)
        kn = kh * lax.rsqrt(jnp.sum(kh * kh, axis=-1, keepdims=True) + EPS)
        bh = beta[:, h:h + 1]
        q_ref[:, lo:hi] = qn.astype(BF16)
        kb_ref[:, lo:hi] = (kn * bh).astype(BF16)
        vb_ref[:, lo:hi] = (vh * bh).astype(BF16)
        kn_all.append(kn)
    kt = jnp.concatenate(kn_all, axis=1).T
    for j in range(tm // DN_CHUNK):
        kt_ref[j] = kt[:, j * DN_CHUNK:(j + 1) * DN_CHUNK].astype(BF16)

    zs_ref[...] = _silu(_dot(hb, w_ref[:, C_Z:C_GU])).astype(BF16)

    gu = _gelu_tanh(_dot(hb, w_ref[:, C_GU:C_GV]))
    gv = _gelu_tanh(_dot(hb, w_ref[:, C_GV:C_BA]))
    mu = jnp.mean(gv, axis=-1, keepdims=True)
    gvc = gv - mu
    var = jnp.mean(gvc * gvc, axis=-1, keepdims=True)
    gvn = (gvc * lax.rsqrt(var + EPS) * lnw_ref[...] + lnb_ref[...]).astype(BF16)
    for j in range(tm // GM_CHUNK):
        r0, r1 = j * GM_CHUNK, (j + 1) * GM_CHUNK
        for gi in range(GM_GROUPS):
            c0, c1 = gi * LANES, (gi + 1) * LANES
            sp = _dot(ws_ref[gi], gvn[r0:r1, c0:c1]) + bs_ref[gi]
            ogm_ref[r0:r1, c0:c1] = (gu[r0:r1, c0:c1] * sp).astype(BF16)


def inproj(x2, nw, sh, sc, w, cw, hp, lnw, lnb, ws, bs, *, seq, tm=512):
    t, d = x2.shape
    nt = t // tm
    tiles_per_seq = seq // tm
    nc = tm // DN_CHUNK
    row = lambda i: (i, 0)
    const2 = lambda i: (0, 0)
    const3 = lambda i: (0, 0, 0)
    bidx = lambda i: (i // tiles_per_seq, 0, 0)
    out_shape = [
        jax.ShapeDtypeStruct((t, DN_WIDTH), BF16),
        jax.ShapeDtypeStruct((t, DN_WIDTH), BF16),
        jax.ShapeDtypeStruct((t, DN_WIDTH), BF16),
        jax.ShapeDtypeStruct((t // DN_CHUNK, DN_WIDTH, DN_CHUNK), BF16),
        jax.ShapeDtypeStruct((t, DN_WIDTH), BF16),
        jax.ShapeDtypeStruct((t, LANES), F32),
        jax.ShapeDtypeStruct((t // DN_CHUNK, 8, DN_CHUNK), F32),
        jax.ShapeDtypeStruct((t, GM_WIDTH), BF16),
    ]
    out_specs = [
        pl.BlockSpec((tm, DN_WIDTH), row),
        pl.BlockSpec((tm, DN_WIDTH), row),
        pl.BlockSpec((tm, DN_WIDTH), row),
        pl.BlockSpec((nc, DN_WIDTH, DN_CHUNK), lambda i: (i, 0, 0)),
        pl.BlockSpec((tm, DN_WIDTH), row),
        pl.BlockSpec((tm, LANES), row),
        pl.BlockSpec((nc, 8, DN_CHUNK), lambda i: (i, 0, 0)),
        pl.BlockSpec((tm, GM_WIDTH), row),
    ]
    in_specs = [
        pl.BlockSpec((tm, d), row),
        pl.BlockSpec((1, d), const2),
        pl.BlockSpec((1, 1, d), bidx),
        pl.BlockSpec((1, 1, d), bidx),
        pl.BlockSpec((d, C_END), const2),
        pl.BlockSpec((CONV_WIDTH, 3 * DN_WIDTH), const2),
        pl.BlockSpec((8, LANES), const2),
        pl.BlockSpec((1, GM_WIDTH), const2),
        pl.BlockSpec((1, GM_WIDTH), const2),
        pl.BlockSpec((GM_GROUPS, GM_CHUNK, GM_CHUNK), const3),
        pl.BlockSpec((GM_GROUPS, GM_CHUNK, LANES), const3),
    ]
    return pl.pallas_call(
        functools.partial(_inproj_kernel, tm=tm, tiles_per_seq=tiles_per_seq),
        grid=(nt,),
        in_specs=in_specs,
        out_specs=out_specs,
        out_shape=out_shape,
        scratch_shapes=[pltpu.VMEM((tm + 8, 3 * DN_WIDTH), F32)],
        compiler_params=_cparams(("arbitrary",)),
        name="inproj",
    )(x2, nw, sh, sc, w, cw, hp, lnw, lnb, ws, bs)


INV_BASE = 8


def _unit_lower_inverse(a_list, ri, ci):
    c = a_list[0].shape[0]
    same = (ri // INV_BASE) == (ci // INV_BASE)
    eye = jnp.where(ri == ci, 1.0, 0.0)
    ad = [jnp.where(same, a, 0.0) for a in a_list]
    x = [eye - a for a in ad]
    pb = [a.astype(BF16) for a in ad]
    p = [_dot(b, b) for b in pb]
    n = 2
    while True:
        pb = [m.astype(BF16) for m in p]
        x = [xi + _dot(xi.astype(BF16), b) for xi, b in zip(x, pb)]
        n *= 2
        if n >= INV_BASE:
            break
        p = [_dot(b, b) for b in pb]
    s = INV_BASE
    while s < c:
        off = jnp.logical_and((ri // (2 * s)) == (ci // (2 * s)), (ri // s) != (ci // s))
        xb = [xi.astype(BF16) for xi in x]
        y = [_dot(b, jnp.where(off, a, 0.0).astype(BF16)).astype(BF16) for b, a in zip(xb, a_list)]
        x = [xi - _dot(yi, b) for xi, yi, b in zip(x, y, xb)]
        s *= 2
    return x


def _deltanet_kernel(q_ref, kb_ref, vb_ref, kt_ref, zs_ref, gc_ref, gct_ref, nw_ref,
                     o_ref, s_ref, u_ref, w_ref, attn_ref, *, tl, unroll):
    l = pl.program_id(1)

    @pl.when(l == 0)
    def _():
        s_ref[...] = jnp.zeros_like(s_ref)

    c = DN_CHUNK
    ri = lax.broadcasted_iota(jnp.int32, (c, c), 0)
    ci = lax.broadcasted_iota(jnp.int32, (c, c), 1)
    tri = ri >= ci
    strict = ri > ci
    nw = nw_ref[...]
    heads = [(bb, h, h * HEAD_DIM, (h + 1) * HEAD_DIM)
             for bb in range(DN_PAIR) for h in range(DN_HEADS)]

    def prep(it, carry):
        items = []
        for sub in range(unroll):
            ch = it * unroll + sub
            r0 = pl.multiple_of(ch * c, c)
            for bb in range(DN_PAIR):
                gc = gc_ref[bb, pl.ds(r0, c), :]
                gct = gct_ref[bb, ch]
                for h in range(DN_HEADS):
                    items.append((bb, ch, r0, h, h * HEAD_DIM, (h + 1) * HEAD_DIM,
                                  jnp.broadcast_to(gc[:, 4 + h:5 + h], (c, c)),
                                  jnp.broadcast_to(gct[4 + h:5 + h, :], (c, c))))
        kts = [kt_ref[bb, ch, lo:hi, :] for bb, ch, r0, h, lo, hi, _, _ in items]
        kbs = [kb_ref[bb, pl.ds(r0, c), lo:hi] for bb, ch, r0, h, lo, hi, _, _ in items]
        decay = [jnp.where(tri, jnp.exp(jnp.minimum(gcol - grow, 0.0)), 0.0)
                 for *_, gcol, grow in items]
        a = [jnp.where(strict, _dot(kb, kt) * dc, 0.0) for kb, kt, dc in zip(kbs, kts, decay)]
        for (bb, ch, r0, h, lo, hi, _, _), kt, dc in zip(items, kts, decay):
            attn_ref[bb, ch, h] = (_dot(q_ref[bb, pl.ds(r0, c), lo:hi], kt) * dc).astype(BF16)
        tb = [t.astype(BF16) for t in _unit_lower_inverse(a, ri, ci)]
        for (bb, ch, r0, h, lo, hi, gcol, _), t, kb in zip(items, tb, kbs):
            u_ref[bb, pl.ds(r0, c), lo:hi] = _dot(t, vb_ref[bb, pl.ds(r0, c), lo:hi])
            w_ref[bb, pl.ds(r0, c), lo:hi] = _dot(
                t, (kb.astype(F32) * jnp.exp(gcol)).astype(BF16)).astype(BF16)
        return carry

    lax.fori_loop(0, tl // (c * unroll), prep, 0)

    def step(ch, carry):
        r0 = pl.multiple_of(ch * c, c)
        gcs = [gc_ref[bb, pl.ds(r0, c), :] for bb in range(DN_PAIR)]
        gcol = [jnp.broadcast_to(gcs[bb][:, 4 + h:5 + h], (c, c)) for bb, h, lo, hi in heads]
        glast = [jnp.broadcast_to(g[c - 1:c, :], (c, c)) for g in gcol]
        q = [q_ref[bb, pl.ds(r0, c), lo:hi] for bb, h, lo, hi in heads]
        sb = [s_ref[bb * DN_HEADS + h].astype(BF16) for bb, h, lo, hi in heads]
        ws = [_dot(w_ref[bb, pl.ds(r0, c), lo:hi], s) for (bb, h, lo, hi), s in zip(heads, sb)]
        qs = [_dot(qh, s) for qh, s in zip(q, sb)]
        v_new = [u_ref[bb, pl.ds(r0, c), lo:hi] - w for (bb, h, lo, hi), w in zip(heads, ws)]
        av = [_dot(attn_ref[bb, ch, h], v.astype(BF16))
              for (bb, h, lo, hi), v in zip(heads, v_new)]
        kv = [_dot(kt_ref[bb, ch, lo:hi, :], (v * jnp.exp(gl - g)).astype(BF16))
              for (bb, h, lo, hi), v, gl, g in zip(heads, v_new, glast, gcol)]
        for (bb, h, lo, hi), upd, gl in zip(heads, kv, glast):
            s_ref[bb * DN_HEADS + h] = s_ref[bb * DN_HEADS + h] * jnp.exp(gl) + upd
        for (bb, h, lo, hi), qsh, avh, g in zip(heads, qs, av, gcol):
            o = qsh * jnp.exp(g) + avh
            ms = jnp.mean(o * o, axis=-1, keepdims=True)
            on = o * lax.rsqrt(ms + EPS) * nw
            zs = zs_ref[bb, pl.ds(r0, c), lo:hi].astype(F32)
            o_ref[bb, pl.ds(r0, c), lo:hi] = (on * zs).astype(BF16)
        return carry

    lax.fori_loop(0, tl // c, step, 0)


DN_PAIR = 2


def deltanet(q, kb, vb, kt, zs, gc, gct, nw, *, batch, seq, tl=1024):
    t = q.shape[0]
    nl = seq // tl
    nc = tl // DN_CHUNK
    assert batch % DN_PAIR == 0
    seqs = seq // DN_CHUNK
    rows = lambda a: a.reshape(batch, seq, a.shape[-1])
    chunks = lambda a: a.reshape(batch, seqs, a.shape[-2], a.shape[-1])
    blk = lambda w: pl.BlockSpec((DN_PAIR, tl, w), lambda b, l: (b, l, 0))
    blk4 = lambda r: pl.BlockSpec((DN_PAIR, nc, r, DN_CHUNK), lambda b, l: (b, l, 0, 0))
    out = pl.pallas_call(
        functools.partial(_deltanet_kernel, tl=tl, unroll=2),
        grid=(batch // DN_PAIR, nl),
        in_specs=[blk(DN_WIDTH), blk(DN_WIDTH), blk(DN_WIDTH), blk4(DN_WIDTH), blk(DN_WIDTH),
                  blk(LANES), blk4(8),
                  pl.BlockSpec((1, HEAD_DIM), lambda b, l: (0, 0))],
        out_specs=blk(DN_WIDTH),
        out_shape=jax.ShapeDtypeStruct((batch, seq, DN_WIDTH), BF16),
        scratch_shapes=[pltpu.VMEM((DN_PAIR * DN_HEADS, HEAD_DIM, HEAD_DIM), F32),
                        pltpu.VMEM((DN_PAIR, tl, DN_WIDTH), F32),
                        pltpu.VMEM((DN_PAIR, tl, DN_WIDTH), BF16),
                        pltpu.VMEM((DN_PAIR, nc, DN_HEADS, DN_CHUNK, DN_CHUNK), BF16)],
        compiler_params=_cparams(("arbitrary", "arbitrary")),
        name="deltanet",
    )(rows(q), rows(kb), rows(vb), chunks(kt), rows(zs), rows(gc), chunks(gct), nw)
    return out.reshape(t, DN_WIDTH)


def _outproj_kernel(odn_ref, ogm_ref, w_ref, x_ref, g_ref, o_ref):
    mix = _dot(odn_ref[...], w_ref[0:DN_WIDTH, :]) + _dot(ogm_ref[...], w_ref[DN_WIDTH:, :])
    o_ref[...] = x_ref[...] + g_ref[0] * mix


def outproj(odn, ogm, w, x2, g1, *, seq, tm=1024):
    t, d = x2.shape
    tiles_per_seq = seq // tm
    row = lambda i: (i, 0)
    return pl.pallas_call(
        _outproj_kernel,
        grid=(t // tm,),
        in_specs=[pl.BlockSpec((tm, DN_WIDTH), row),
                  pl.BlockSpec((tm, GM_WIDTH), row),
                  pl.BlockSpec((d, d), lambda i: (0, 0)),
                  pl.BlockSpec((tm, d), row),
                  pl.BlockSpec((1, 1, d), lambda i: (i // tiles_per_seq, 0, 0))],
        out_specs=pl.BlockSpec((tm, d), row),
        out_shape=jax.ShapeDtypeStruct((t, d), F32),
        compiler_params=_cparams(("arbitrary",)),
        name="outproj",
    )(odn, ogm, w, x2, g1)


def _ffn_kernel(x_ref, nw_ref, sh_ref, sc_ref, g_ref, wgu_ref, wd_ref, o_ref,
                hb_ref, acc_ref, *, tf):
    f = pl.program_id(1)

    @pl.when(f == 0)
    def _():
        hb_ref[...] = _norm_mod(x_ref[...], nw_ref[...], sh_ref[0], sc_ref[0]).astype(BF16)
        acc_ref[...] = jnp.zeros_like(acc_ref)

    gu = _dot(hb_ref[...], wgu_ref[0])
    acc_ref[...] += _dot((_silu(gu[:, :tf]) * gu[:, tf:]).astype(BF16), wd_ref[...])

    @pl.when(f == pl.num_programs(1) - 1)
    def _():
        o_ref[...] = x_ref[...] + g_ref[0] * acc_ref[...]


FFN_TF = 1408


def pack_gate_up(wg, wu, tf):
    d, dff = wg.shape
    nf = dff // tf
    gu = jnp.concatenate([wg.reshape(d, nf, tf), wu.reshape(d, nf, tf)], axis=-1)
    return jnp.swapaxes(gu, 0, 1)


def ffn(x2, nw, sh, sc, g2, wgu, wd, *, seq, tm=512):
    t, d = x2.shape
    nf, _, tf2 = wgu.shape
    tf = tf2 // 2
    tiles_per_seq = seq // tm
    row = lambda i, f: (i, 0)
    bidx = lambda i, f: (i // tiles_per_seq, 0, 0)
    return pl.pallas_call(
        functools.partial(_ffn_kernel, tf=tf),
        grid=(t // tm, nf),
        in_specs=[pl.BlockSpec((tm, d), row),
                  pl.BlockSpec((1, d), lambda i, f: (0, 0)),
                  pl.BlockSpec((1, 1, d), bidx),
                  pl.BlockSpec((1, 1, d), bidx),
                  pl.BlockSpec((1, 1, d), bidx),
                  pl.BlockSpec((1, d, tf2), lambda i, f: (f, 0, 0)),
                  pl.BlockSpec((tf, d), lambda i, f: (f, 0))],
        out_specs=pl.BlockSpec((tm, d), row),
        out_shape=jax.ShapeDtypeStruct((t, d), F32),
        scratch_shapes=[pltpu.VMEM((tm, d), BF16), pltpu.VMEM((tm, d), F32)],
        compiler_params=_cparams(("arbitrary", "arbitrary")),
        name="ffn",
    )(x2, nw, sh, sc, g2, wgu, wd)


def _router_kernel(x_ref, nw_ref, sh_ref, sc_ref, wr_ref, h_ref, meta_ref, cnt_ref,
                   carry_ref, *, tm):
    i = pl.program_id(0)

    @pl.when(i == 0)
    def _():
        carry_ref[...] = jnp.zeros_like(carry_ref)

    h = _norm_mod(x_ref[...], nw_ref[...], sh_ref[0], sc_ref[0])
    for k in range(ROW_TILES):
        h_ref[pl.ds(k, tm, stride=ROW_TILES), :] = h[:, k * LANES:(k + 1) * LANES]
    h_hi, h_lo = _split_hi_lo(h)
    w_hi, w_lo = _split_hi_lo(wr_ref[...])
    logits = _dot(h_hi, w_hi) + _dot(h_lo, w_hi) + _dot(h_hi, w_lo)
    lane = lax.broadcasted_iota(jnp.int32, (tm, LANES), 1).astype(F32)
    neg = jnp.float32(-jnp.inf)
    lg = jnp.where(lane < N_EXPERTS, logits, neg)
    m1 = jnp.max(lg, axis=-1, keepdims=True)
    i1 = jnp.min(jnp.where(lg == m1, lane, float(LANES)), axis=-1, keepdims=True)
    lg2 = jnp.where(lane == i1, neg, lg)
    m2 = jnp.max(lg2, axis=-1, keepdims=True)
    i2 = jnp.min(jnp.where(lg2 == m2, lane, float(LANES)), axis=-1, keepdims=True)
    e2 = jnp.exp(m2 - m1)
    w1 = 1.0 / (1.0 + e2)
    w2 = e2 / (1.0 + e2)
    oh1 = lane == i1
    oh2 = lane == i2
    onehot = jnp.where(oh1 | oh2, 1.0, 0.0)
    ri = lax.broadcasted_iota(jnp.int32, (tm, tm), 0)
    ci = lax.broadcasted_iota(jnp.int32, (tm, tm), 1)
    lower = jnp.where(ri > ci, 1.0, 0.0).astype(BF16)
    before = _dot(lower, onehot.astype(BF16)) + carry_ref[0:1, :]
    r1 = jnp.sum(jnp.where(oh1, before, 0.0), axis=-1, keepdims=True)
    r2 = jnp.sum(jnp.where(oh2, before, 0.0), axis=-1, keepdims=True)
    total = carry_ref[0:1, :] + jnp.sum(onehot, axis=0, keepdims=True)
    carry_ref[...] = jnp.broadcast_to(total, carry_ref.shape)
    cnt_ref[...] = jnp.broadcast_to(total, cnt_ref.shape)
    meta = jnp.where(lane == 0, i1, 0.0)
    meta = jnp.where(lane == 1, i2, meta)
    meta = jnp.where(lane == 2, w1, meta)
    meta = jnp.where(lane == 3, w2, meta)
    meta = jnp.where(lane == 4, r1, meta)
    meta = jnp.where(lane == 5, r2, meta)
    meta_ref[...] = meta


def router(x2, nw, sh, sc, wr, *, seq, tm=512):
    t, d = x2.shape
    tiles_per_seq = seq // tm
    row = lambda i: (i, 0)
    bidx = lambda i: (i // tiles_per_seq, 0, 0)
    return pl.pallas_call(
        functools.partial(_router_kernel, tm=tm),
        grid=(t // tm,),
        in_specs=[pl.BlockSpec((tm, d), row),
                  pl.BlockSpec((1, d), lambda i: (0, 0)),
                  pl.BlockSpec((1, 1, d), bidx),
                  pl.BlockSpec((1, 1, d), bidx),
                  pl.BlockSpec((d, LANES), lambda i: (0, 0))],
        out_specs=[pl.BlockSpec((tm * ROW_TILES, LANES), row),
                   pl.BlockSpec((tm, LANES), row),
                   pl.BlockSpec((8, LANES), lambda i: (0, 0))],
        out_shape=[jax.ShapeDtypeStruct((t * ROW_TILES, LANES), F32),
                   jax.ShapeDtypeStruct((t, LANES), F32),
                   jax.ShapeDtypeStruct((8, LANES), F32)],
        scratch_shapes=[pltpu.VMEM((8, LANES), F32)],
        compiler_params=_cparams(("arbitrary",)),
        name="router",
    )(x2, nw, sh, sc, wr)


def _experts_kernel(te_ref, tv_ref, src_cur, src_next, dst_prev, dst_cur, h_hbm,
                    wg_ref, wu_ref, wd_ref, ys_hbm,
                    xbuf, xb_ref, acc_ref, obuf, gate_ref, act_ref, gsem, ssem, *, tm, rows):
    i = pl.program_id(0)
    f = pl.program_id(1)
    last_i = pl.num_programs(0) - 1
    last_f = pl.num_programs(1) - 1
    valid = tv_ref[i] == 1

    def gather_row(idx_ref, r):
        t = idx_ref[0, 0, r]
        pltpu.make_async_copy(h_hbm.at[_token_rows(t), :], xbuf.at[_token_rows(r), :], gsem).start()

    def scatter_row(idx_ref, r):
        d = idx_ref[0, 0, r]
        pltpu.make_async_copy(obuf.at[_token_rows(r), :], ys_hbm.at[_token_rows(d), :], ssem).start()

    def wait_gather():
        pltpu.make_async_copy(h_hbm.at[pl.ds(0, tm * ROW_TILES), :], xbuf, gsem).wait()

    def wait_scatter():
        pltpu.make_async_copy(obuf, ys_hbm.at[pl.ds(0, tm * ROW_TILES), :], ssem).wait()

    @pl.when(f == 0)
    def _():
        @pl.when(i == 0)
        def _():
            obuf[...] = jnp.zeros_like(obuf)

            def first(r, carry):
                gather_row(src_cur, r)
                return carry

            lax.fori_loop(0, tm, first, 0)

        wait_gather()
        for k in range(ROW_TILES):
            xb_ref[:, k * LANES:(k + 1) * LANES] = xbuf[pl.ds(k, tm, stride=ROW_TILES), :].astype(BF16)
        acc_ref[...] = jnp.zeros_like(acc_ref)

    def move_rows(lo, hi):
        for j in range(lo, hi):
            gather_row(src_next, f * rows + j)
            scatter_row(dst_prev, f * rows + j)

    cut1, cut2 = rows // 4, rows // 2

    def once(block):
        def body(_, carry):
            block()
            return carry

        lax.fori_loop(0, tv_ref[i], body, 0)

    def block_gate():
        move_rows(0, cut1)
        gate_ref[...] = _dot(xb_ref[...], wg_ref[0])

    def block_up():
        move_rows(cut1, cut2)
        up = _dot(xb_ref[...], wu_ref[0])
        act_ref[...] = (_silu(gate_ref[...]) * up).astype(BF16)

    def block_down():
        move_rows(cut2, rows)
        acc_ref[...] += _dot(act_ref[...], wd_ref[0])

    once(block_gate)
    once(block_up)
    once(block_down)

    @pl.when(jnp.logical_not(valid))
    def _():
        move_rows(0, rows)

    @pl.when(f == last_f)
    def _():
        wait_scatter()
        for k in range(ROW_TILES):
            obuf[pl.ds(k, tm, stride=ROW_TILES), :] = acc_ref[:, k * LANES:(k + 1) * LANES]

        @pl.when(i == last_i)
        def _():
            def final(r, carry):
                scatter_row(dst_cur, r)
                return carry

            lax.fori_loop(0, tm, final, 0)
            wait_scatter()
            wait_gather()


def experts(tile_e, tile_v, src3, dst3, h, wg, wu, wd, *, tm, tf=1792):
    nt = src3.shape[0]
    d = D_MODEL
    flat = (tm * ROW_TILES, LANES)
    dff = wg.shape[2]
    nf = dff // tf
    rows = tm // nf
    assert rows * nf == tm

    def fsel(i, f, tv):
        return jnp.where(tv[i] == 1, f, nf - 1)

    smem = functools.partial(pl.BlockSpec, (1, 1, tm), memory_space=pltpu.SMEM)
    grid_spec = pltpu.PrefetchScalarGridSpec(
        num_scalar_prefetch=2,
        grid=(nt, nf),
        in_specs=[smem(lambda i, f, te, tv: (i, 0, 0)),
                  smem(lambda i, f, te, tv: (jnp.minimum(i + 1, nt - 1), 0, 0)),
                  smem(lambda i, f, te, tv: (jnp.maximum(i - 1, 0), 0, 0)),
                  smem(lambda i, f, te, tv: (i, 0, 0)),
                  pl.BlockSpec(memory_space=pl.ANY),
                  pl.BlockSpec((1, d, tf), lambda i, f, te, tv: (te[i], 0, fsel(i, f, tv))),
                  pl.BlockSpec((1, d, tf), lambda i, f, te, tv: (te[i], 0, fsel(i, f, tv))),
                  pl.BlockSpec((1, tf, d), lambda i, f, te, tv: (te[i], fsel(i, f, tv), 0))],
        out_specs=pl.BlockSpec(memory_space=pl.ANY),
        scratch_shapes=[pltpu.VMEM(flat, F32), pltpu.VMEM((tm, d), BF16),
                        pltpu.VMEM((tm, d), F32), pltpu.VMEM(flat, F32),
                        pltpu.VMEM((tm, tf), F32), pltpu.VMEM((tm, tf), BF16),
                        pltpu.SemaphoreType.DMA(()), pltpu.SemaphoreType.DMA(())],
    )
    return pl.pallas_call(
        functools.partial(_experts_kernel, tm=tm, rows=rows),
        grid_spec=grid_spec,
        out_shape=jax.ShapeDtypeStruct((nt * tm * ROW_TILES, LANES), F32),
        compiler_params=_cparams(("arbitrary", "arbitrary")),
        name="experts",
    )(tile_e, tile_v, src3, src3, dst3, dst3, h, wg, wu, wd)


def _combine_kernel(x_ref, meta_ref, g_ref, y0_ref, y1_ref, *rest, tm):
    o_ref = rest[-1]
    meta = meta_ref[...]
    w0, w1 = meta[:, 2:3], meta[:, 3:4]
    for k in range(ROW_TILES):
        cols = slice(k * LANES, (k + 1) * LANES)
        rows = pl.ds(k, tm, stride=ROW_TILES)
        y = w0 * y0_ref[rows, :] + w1 * y1_ref[rows, :]
        o_ref[:, cols] = x_ref[:, cols] + g_ref[0, :, cols] * y
    if len(rest) == 4:
        nw_ref, sh_ref, sc_ref = rest[:3]
        o_ref[...] = _norm_mod(o_ref[...], nw_ref[...], sh_ref[0], sc_ref[0])


def combine(x2, meta, g2, ys, *, seq, final=None, tm=512):
    t, d = x2.shape
    tiles_per_seq = seq // tm
    nt = t // tm
    row = lambda i: (i, 0)
    bidx = lambda i: (i // tiles_per_seq, 0, 0)
    in_specs = [pl.BlockSpec((tm, d), row),
                pl.BlockSpec((tm, LANES), row),
                pl.BlockSpec((1, 1, d), bidx),
                pl.BlockSpec((tm * ROW_TILES, LANES), row),
                pl.BlockSpec((tm * ROW_TILES, LANES), lambda i: (nt + i, 0))]
    args = [x2, meta, g2, ys, ys]
    if final is not None:
        in_specs += [pl.BlockSpec((1, d), lambda i: (0, 0)), pl.BlockSpec((1, 1, d), bidx),
                     pl.BlockSpec((1, 1, d), bidx)]
        args += list(final)
    return pl.pallas_call(
        functools.partial(_combine_kernel, tm=tm),
        grid=(nt,),
        in_specs=in_specs,
        out_specs=pl.BlockSpec((tm, d), row),
        out_shape=jax.ShapeDtypeStruct((t, d), F32),
        compiler_params=_cparams(("arbitrary",)),
        name="combine",
    )(*args)


def _final_kernel(x_ref, nw_ref, sh_ref, sc_ref, o_ref):
    o_ref[...] = _norm_mod(x_ref[...], nw_ref[...], sh_ref[0], sc_ref[0])


def final_norm(x2, nw, sh, sc, *, seq, tm=1024):
    t, d = x2.shape
    tiles_per_seq = seq // tm
    row = lambda i: (i, 0)
    bidx = lambda i: (i // tiles_per_seq, 0, 0)
    return pl.pallas_call(
        _final_kernel,
        grid=(t // tm,),
        in_specs=[pl.BlockSpec((tm, d), row),
                  pl.BlockSpec((1, d), lambda i: (0, 0)),
                  pl.BlockSpec((1, 1, d), bidx),
                  pl.BlockSpec((1, 1, d), bidx)],
        out_specs=pl.BlockSpec((tm, d), row),
        out_shape=jax.ShapeDtypeStruct((t, d), F32),
        compiler_params=_cparams(("arbitrary",)),
        name="final_norm",
    )(x2, nw, sh, sc)


def _moe_layer(x2, nw, sh, sc, g2, wr, wg, wu, wd, *, seq, tm_e, final=None):
    t, d = x2.shape
    wr_pad = jnp.zeros((d, LANES), F32).at[:, :N_EXPERTS].set(wr)
    h, meta, cnt = router(x2, nw, sh, sc, wr_pad, seq=seq)
    counts = cnt[0, :N_EXPERTS].astype(jnp.int32)
    tiles = (counts + tm_e - 1) // tm_e
    tile_end = jnp.cumsum(tiles)
    offs = (tile_end - tiles) * tm_e
    idx = meta[:, 0:2].astype(jnp.int32)
    rank = meta[:, 4:6].astype(jnp.int32)
    dest = (offs[idx] + rank).reshape(-1)
    n_tiles = -(-(2 * t) // tm_e) + N_EXPERTS
    p = n_tiles * tm_e
    tid = jnp.arange(n_tiles, dtype=jnp.int32)
    tile_e = jnp.minimum(jnp.sum(tid[:, None] >= tile_end[None, :], axis=1), N_EXPERTS - 1)
    tile_v = (tid < tile_end[-1]).astype(jnp.int32)
    tile_e = jnp.where(tile_v == 1, tile_e, tile_e[jnp.maximum(tile_end[-1] - 1, 0)]).astype(jnp.int32)
    entry = jnp.zeros((p,), jnp.int32).at[dest].set(
        jnp.arange(1, 2 * t + 1, dtype=jnp.int32), unique_indices=True, mode="promise_in_bounds") - 1
    is_pad = entry < 0
    pad_rank = jnp.cumsum(is_pad.astype(jnp.int32)) - 1
    src = jnp.where(is_pad, 0, entry // 2)
    dst = jnp.where(is_pad, 2 * t + pad_rank, (entry % 2) * t + entry // 2)
    ys = experts(tile_e, tile_v, src.reshape(n_tiles, 1, tm_e), dst.reshape(n_tiles, 1, tm_e),
                 h, wg, wu, wd, tm=tm_e)
    return combine(x2, meta, g2, ys, seq=seq, final=final)


def kernel(x, c, ada_w, ada_b, norm1_w, norm2_w, w_in, conv_w, a_log, dt_bias, dn_norm_w, gm_ln_w, gm_ln_b, gm_spatial_w, gm_spatial_b, w_out, ffn_w_gate, ffn_w_up, ffn_w_down, moe_router, moe_w_gate, moe_w_up, moe_w_down, final_ada_w, final_ada_b, final_norm_w):
    bsz, seq, d = x.shape
    depth = ada_w.shape[0]
    t = bsz * seq
    assert d == D_MODEL and seq % 1024 == 0
    x2 = x.reshape(t, d)

    mods = ada_mod(c, ada_w, ada_b[:, None, :])
    fmod = ada_mod(c, final_ada_w[None], final_ada_b[None, None, :])[0]

    w_in_p = jnp.concatenate(
        [w_in[:, :, 0:2048], w_in[:, :, 2056:3080], w_in[:, :, 2048:2056],
         jnp.zeros((depth, d, LANES - 8), F32)], axis=-1).astype(BF16)
    w_out_b = w_out.astype(BF16)
    causal = jnp.tril(jnp.ones((GM_CHUNK, GM_CHUNK), dtype=bool))
    ws = jnp.where(causal, gm_spatial_w, 0.0).astype(BF16)
    bs = jnp.broadcast_to(gm_spatial_b[..., None], (depth, GM_GROUPS, GM_CHUNK, LANES))
    hp = jnp.zeros((depth, 8, LANES), F32)
    hp = hp.at[:, 0, 4:8].set(-jnp.exp(a_log)).at[:, 1, 4:8].set(dt_bias)
    ffn_gu = [pack_gate_up(ffn_w_gate[j].astype(BF16), ffn_w_up[j].astype(BF16), FFN_TF)
              for j in range(ffn_w_gate.shape[0])]
    ffn_d = ffn_w_down.astype(BF16)
    moe_g, moe_u, moe_d = moe_w_gate.astype(BF16), moe_w_up.astype(BF16), moe_w_down.astype(BF16)

    fm = fmod.reshape(bsz, 2, 1, d)
    final = (final_norm_w[None], fm[:, 0], fm[:, 1])

    for i in range(depth):
        mod = mods[i].reshape(bsz, 6, 1, d)
        sh1, sc1, g1, sh2, sc2, g2 = (mod[:, k] for k in range(6))
        q, kb, vb, kt, zs, gc, gct, ogm = inproj(
            x2, norm1_w[i][None], sh1, sc1, w_in_p[i], conv_w[i], hp[i],
            gm_ln_w[i][None], gm_ln_b[i][None], ws[i], bs[i], seq=seq)
        odn = deltanet(q, kb, vb, kt, zs, gc, gct, dn_norm_w[i][None], batch=bsz, seq=seq)
        x2 = outproj(odn, ogm, w_out_b[i], x2, g1, seq=seq)
        j = i // 2
        if i % 2 == 0:
            x2 = ffn(x2, norm2_w[i][None], sh2, sc2, g2, ffn_gu[j], ffn_d[j], seq=seq)
        else:
            x2 = _moe_layer(x2, norm2_w[i][None], sh2, sc2, g2, moe_router[j],
                            moe_g[j], moe_u[j], moe_d[j], seq=seq, tm_e=896,
                            final=final if i == depth - 1 else None)
    out = x2 if depth % 2 == 0 else final_norm(x2, *final, seq=seq)
    return out.reshape(bsz, seq, d)
```
